```python
import jax, jax.numpy as jnp
from jax import lax
import numpy as np

D_MODEL = 1024
BATCH = 16
SEQ = 2048
DEPTH = 4
DEC_BATCH = 4
DEC_SEQ = 4096
PAST_LEN = 128

N_MEM = 256
D_FF = 2816
D_CONV = D_MODEL // 2
CONV_W = 3
GLA_HEADS = 4
D_GLA_V = D_MODEL // 2
HEAD_V = D_GLA_V // GLA_HEADS
D_GLA_K = D_GLA_V // 2
HEAD_K = D_GLA_K // GLA_HEADS
GATE_RANK = 16
GATE_TEMP = 16.0
GATE_BIAS_MEAN = 2.0
CHUNK = 64
XA_HEADS = 4
XA_HEAD_DIM = D_MODEL // XA_HEADS
LN_EPS = 1e-5
RMS_EPS = 1e-6
DN_ALPHA = (2 * DEPTH) ** 0.25
DN_BETA = (8 * DEPTH) ** -0.25

IN_SPLITS = (D_CONV, D_CONV, D_CONV, D_GLA_K, D_GLA_K, D_GLA_V, D_GLA_V, GATE_RANK, GATE_RANK, D_MODEL, D_MODEL)
IN_OFFSETS = tuple(int(o) for o in np.cumsum(IN_SPLITS)[:-1])
D_IN = int(sum(IN_SPLITS))

kernel_name = 'hybrid_conv_gla_memory_encoder'


def layer_norm(x, g, b):
    xf = x.astype(jnp.float32)
    mu = jnp.mean(xf, axis=-1, keepdims=True)
    var = jnp.mean(jnp.square(xf - mu), axis=-1, keepdims=True)
    return ((xf - mu) * lax.rsqrt(var + LN_EPS) * g.astype(jnp.float32) + b.astype(jnp.float32)).astype(x.dtype)


def swiglu(x, w_gu, w_down):
    gate, up = jnp.split(x @ w_gu, 2, axis=-1)
    return (jax.nn.silu(gate) * up) @ w_down


def short_conv(u, w):
    up = jnp.pad(u, ((0, 0), (1, 1), (0, 0)))
    return up[:, :-2] * w[0] + up[:, 1:-1] * w[1] + up[:, 2:] * w[2]


def gla_direction(q, k, v, log_a, include_diag):
    B, T, H, DK = q.shape
    DV = v.shape[-1]
    N = T // CHUNK
    f32 = jnp.float32
    qc = q.astype(f32).reshape(B, N, CHUNK, H, DK)
    kc = k.astype(f32).reshape(B, N, CHUNK, H, DK)
    vc = v.astype(f32).reshape(B, N, CHUNK, H, DV)
    b = jnp.cumsum(log_a.astype(f32).reshape(B, N, CHUNK, H, DK), axis=2)
    ref = b[:, :, CHUNK // 2 - 1:CHUNK // 2]
    q_in = qc * jnp.exp(b - ref)
    k_in = kc * jnp.exp(ref - b)
    mask = jnp.tril(jnp.ones((CHUNK, CHUNK), dtype=bool), 0 if include_diag else -1)
    att = jnp.where(mask, jnp.einsum('bnihd,bnjhd->bnhij', q_in, k_in), 0.0)
    o = jnp.einsum('bnhij,bnjhe->bnihe', att, vc)
    b_last = b[:, :, -1]
    kv = jnp.einsum('bnjhd,bnjhe->bnhde', kc * jnp.exp(b_last[:, :, None] - b), vc)

    def step(S, inp):
        kv_n, d_n = inp
        return d_n[..., None] * S + kv_n, S

    S0 = jnp.zeros((B, H, DK, DV), f32)
    _, S_prev = lax.scan(step, S0, (jnp.moveaxis(kv, 1, 0), jnp.moveaxis(jnp.exp(b_last), 1, 0)))
    S_prev = jnp.moveaxis(S_prev, 0, 1)
    o = o + jnp.einsum('bnihd,bnhde->bnihe', qc * jnp.exp(b), S_prev)
    return o.reshape(B, T, H, DV)


def parallel_mixer(x, w_in, conv_w, w_conv_out, gla_w2, gla_b, gla_norm_g, w_gla_out, w_out):
    B, T, _ = x.shape
    h, gb, gc, q, k, v, g, lr_f, lr_b, m_conv, m_gla = jnp.split(x @ w_in, IN_OFFSETS, axis=-1)
    y_conv = (gb * short_conv(gc * h, conv_w)) @ w_conv_out
    q = q.reshape(B, T, GLA_HEADS, HEAD_K) * (HEAD_K ** -0.5)
    k = k.reshape(B, T, GLA_HEADS, HEAD_K)
    v = v.reshape(B, T, GLA_HEADS, HEAD_V)
    log_a_f = (jax.nn.log_sigmoid((lr_f @ gla_w2[0] + gla_b[0]).astype(jnp.float32)) / GATE_TEMP).reshape(B, T, GLA_HEADS, HEAD_K)
    log_a_b = (jax.nn.log_sigmoid((lr_b @ gla_w2[1] + gla_b[1]).astype(jnp.float32)) / GATE_TEMP).reshape(B, T, GLA_HEADS, HEAD_K)
    o_f = gla_direction(q, k, v, log_a_f, True)
    o_b = jnp.flip(gla_direction(jnp.flip(q, 1), jnp.flip(k, 1), jnp.flip(v, 1), jnp.flip(log_a_b, 1), False), 1)
    o = o_f + o_b
    o = o * lax.rsqrt(jnp.mean(jnp.square(o), axis=-1, keepdims=True) + RMS_EPS) * gla_norm_g.astype(jnp.float32)
    o = o.astype(x.dtype).reshape(B, T, D_GLA_V) * jax.nn.silu(g)
    y_gla = o @ w_gla_out
    merged = jax.nn.sigmoid(m_conv) * y_conv + jax.nn.sigmoid(m_gla) * y_gla
    return merged @ w_out


def memory_cross_attention(x, mem, w_q, w_kv, w_o):
    B, T, _ = x.shape
    M = mem.shape[1]
    q = (x @ w_q).reshape(B, T, XA_HEADS, XA_HEAD_DIM)
    k, v = jnp.split(mem @ w_kv, 2, axis=-1)
    k = k.reshape(B, M, XA_HEADS, XA_HEAD_DIM)
    v = v.reshape(B, M, XA_HEADS, XA_HEAD_DIM)
    s = jnp.einsum('bthd,bmhd->bhtm', q, k).astype(jnp.float32) * (XA_HEAD_DIM ** -0.5)
    p = jax.nn.softmax(s, axis=-1).astype(v.dtype)
    o = jnp.einsum('bhtm,bmhd->bthd', p, v).reshape(B, T, D_MODEL)
    return o @ w_o


def trunk(x, mem, ffn1_w_gu, ffn1_w_down, w_mix_in, conv_w, w_conv_out, gla_gate_w2, gla_gate_b,
          gla_norm_g, w_gla_out, w_mix_out, xa_w_q, xa_w_kv, xa_w_o, ffn2_w_gu, ffn2_w_down, ln_g, ln_b):
    for l in range(DEPTH):
        x = layer_norm(DN_ALPHA * x + 0.5 * swiglu(x, ffn1_w_gu[l], ffn1_w_down[l]), ln_g[l, 0], ln_b[l, 0])
        x = layer_norm(DN_ALPHA * x + parallel_mixer(x, w_mix_in[l], conv_w[l], w_conv_out[l], gla_gate_w2[l],
                                                     gla_gate_b[l], gla_norm_g[l], w_gla_out[l], w_mix_out[l]),
                       ln_g[l, 1], ln_b[l, 1])
        x = layer_norm(DN_ALPHA * x + memory_cross_attention(x, mem, xa_w_q[l], xa_w_kv[l], xa_w_o[l]), ln_g[l, 2], ln_b[l, 2])
        x = layer_norm(DN_ALPHA * x + 0.5 * swiglu(x, ffn2_w_gu[l], ffn2_w_down[l]), ln_g[l, 3], ln_b[l, 3])
    return x


def _normal(key, shape, scale):
    return jax.random.normal(key, shape, jnp.float32) * scale


def setup_inputs(seed: int = 0) -> dict:
    key = jax.random.key(seed)
    ks = jax.random.split(key, 24)
    L = DEPTH
    return {
        'x_prompt': _normal(ks[0], (BATCH, SEQ, D_MODEL), 1.0),
        'x_sample': _normal(ks[1], (DEC_BATCH, DEC_SEQ, D_MODEL), 1.0),
        'mem_prompt': _normal(ks[2], (BATCH, N_MEM, D_MODEL), 1.0),
        'mem_sample': _normal(ks[3], (DEC_BATCH, N_MEM, D_MODEL), 1.0),
        'ffn1_w_gu': _normal(ks[4], (L, D_MODEL, 2 * D_FF), D_MODEL ** -0.5),
        'ffn1_w_down': _normal(ks[5], (L, D_FF, D_MODEL), DN_BETA * D_FF ** -0.5),
        'w_mix_in': _normal(ks[6], (L, D_MODEL, D_IN), D_MODEL ** -0.5),
        'conv_w': _normal(ks[7], (L, CONV_W, D_CONV), CONV_W ** -0.5),
        'w_conv_out': _normal(ks[8], (L, D_CONV, D_MODEL), D_CONV ** -0.5),
        'gla_gate_w2': _normal(ks[9], (L, 2, GATE_RANK, D_GLA_K), GATE_RANK ** -0.5),
        'gla_gate_b': GATE_BIAS_MEAN + _normal(ks[10], (L, 2, D_GLA_K), 0.1),
        'gla_norm_g': 1.0 + _normal(ks[11], (L, HEAD_V), 0.01),
        'w_gla_out': _normal(ks[12], (L, D_GLA_V, D_MODEL), D_GLA_V ** -0.5),
        'w_mix_out': _normal(ks[13], (L, D_MODEL, D_MODEL), DN_BETA * D_MODEL ** -0.5),
        'xa_w_q': _normal(ks[14], (L, D_MODEL, D_MODEL), D_MODEL ** -0.5),
        'xa_w_kv': _normal(ks[15], (L, D_MODEL, 2 * D_MODEL), D_MODEL ** -0.5),
        'xa_w_o': _normal(ks[16], (L, D_MODEL, D_MODEL), DN_BETA * D_MODEL ** -0.5),
        'ffn2_w_gu': _normal(ks[17], (L, D_MODEL, 2 * D_FF), D_MODEL ** -0.5),
        'ffn2_w_down': _normal(ks[18], (L, D_FF, D_MODEL), DN_BETA * D_FF ** -0.5),
        'ln_g': 1.0 + _normal(ks[19], (L, 4, D_MODEL), 0.01),
        'ln_b': _normal(ks[20], (L, 4, D_MODEL), 0.01),
    }


def reference(x_prompt, x_sample, mem_prompt, mem_sample, ffn1_w_gu, ffn1_w_down, w_mix_in, conv_w, w_conv_out,
              gla_gate_w2, gla_gate_b, gla_norm_g, w_gla_out, w_mix_out, xa_w_q, xa_w_kv, xa_w_o,
              ffn2_w_gu, ffn2_w_down, ln_g, ln_b):
    weights = (ffn1_w_gu, ffn1_w_down, w_mix_in, conv_w, w_conv_out, gla_gate_w2, gla_gate_b, gla_norm_g,
               w_gla_out, w_mix_out, xa_w_q, xa_w_kv, xa_w_o, ffn2_w_gu, ffn2_w_down, ln_g, ln_b)
    y_prompt = trunk(x_prompt, mem_prompt, *weights)
    y_sample = trunk(x_sample, mem_sample, *weights)
    return (y_prompt, y_sample)
```

```python
import functools

import jax
import jax.numpy as jnp
from jax import lax
from jax.experimental import pallas as pl
from jax.experimental.pallas import tpu as pltpu

F32 = jnp.float32
BF16 = jnp.bfloat16

D_MODEL = 1024
DEPTH = 4
D_FF = 2816
D_CONV = D_MODEL // 2
GLA_HEADS = 4
D_GLA_V = D_MODEL // 2
HEAD_V = D_GLA_V // GLA_HEADS
D_GLA_K = D_GLA_V // 2
HEAD_K = D_GLA_K // GLA_HEADS
GATE_RANK = 16
GATE_TEMP = 16.0
CHUNK = 64
XA_HEADS = 4
XA_HEAD_DIM = D_MODEL // XA_HEADS
LN_EPS = 1e-5
RMS_EPS = 1e-6
DN_ALPHA = (2 * DEPTH) ** 0.25

SUBLANES = 8
LANES = 128
VMEM_LIMIT_BYTES = 56 * 1024 * 1024

TOKEN_TILE = 512
TIME_TILE = 512
FF_CHUNK = 1408
LR_PAD = LANES

_H0, _GC0, _GB0 = 0, D_CONV, 2 * D_CONV
_Q0 = 3 * D_CONV
_K0 = _Q0 + D_GLA_K
_V0 = _K0 + D_GLA_K
_G0 = _V0 + D_GLA_V
D_PROJ = _G0 + D_GLA_V


def _dot(a, b):
    return jnp.dot(a, b, preferred_element_type=F32)


def _dot_nt(a, b):
    return lax.dot_general(a, b, (((1,), (1,)), ((), ())), preferred_element_type=F32)


def _residual_layer_norm(x, y, g, b):
    z = DN_ALPHA * x + y
    mu = jnp.mean(z, axis=-1, keepdims=True)
    zc = z - mu
    var = jnp.mean(zc * zc, axis=-1, keepdims=True)
    return zc * lax.rsqrt(var + LN_EPS) * g + b


def _const_spec(shape):
    n = len(shape)
    return pl.BlockSpec(shape, lambda *_: (0,) * n)


def _ffn_kernel(x_ref, wgu_ref, wd_ref, g_ref, b_ref, o_ref):
    x = x_ref[...]
    xb = x.astype(BF16)
    acc = jnp.zeros(x.shape, F32)
    for c0 in range(0, D_FF, FF_CHUNK):
        gate = _dot(xb, wgu_ref[:, c0:c0 + FF_CHUNK])
        up = _dot(xb, wgu_ref[:, D_FF + c0:D_FF + c0 + FF_CHUNK])
        act = (gate * jax.nn.sigmoid(gate)) * up
        acc = acc + _dot(act.astype(BF16), wd_ref[c0:c0 + FF_CHUNK, :])
    o_ref[...] = _residual_layer_norm(x, 0.5 * acc, g_ref[...], b_ref[...])


def _ffn_call(x2d, w_gu, w_down, ln_g, ln_b):
    n = x2d.shape[0]
    assert n % TOKEN_TILE == 0
    return pl.pallas_call(
        _ffn_kernel,
        grid=(n // TOKEN_TILE,),
        in_specs=[
            pl.BlockSpec((TOKEN_TILE, D_MODEL), lambda i: (i, 0)),
            _const_spec((D_MODEL, 2 * D_FF)),
            _const_spec((D_FF, D_MODEL)),
            _const_spec((1, D_MODEL)),
            _const_spec((1, D_MODEL)),
        ],
        out_specs=pl.BlockSpec((TOKEN_TILE, D_MODEL), lambda i: (i, 0)),
        out_shape=jax.ShapeDtypeStruct((n, D_MODEL), F32),
        compiler_params=pltpu.CompilerParams(
            dimension_semantics=("arbitrary",), vmem_limit_bytes=VMEM_LIMIT_BYTES),
        name="ffn_ln",
    )(x2d, w_gu, w_down, ln_g, ln_b)


def _log_decay(xb, wlr_ref, w2_ref, b2_ref):
    lr = _dot(xb, wlr_ref[...])
    z = _dot(lr.astype(BF16), w2_ref[...]) + b2_ref[...]
    return jax.nn.log_sigmoid(z) / GATE_TEMP


def _gla_tile(q_ref, k_ref, v_ref, la_ref, o_ref, st_ref, *, q0, k0, v0, reverse):
    rows = q_ref.shape[0]
    n_chunks = rows // CHUNK
    ri = lax.broadcasted_iota(jnp.int32, (CHUNK, CHUNK), 0)
    ci = lax.broadcasted_iota(jnp.int32, (CHUNK, CHUNK), 1)
    if reverse:
        cum = (ci >= ri).astype(BF16)
        att_mask = ci > ri
        ref_row, tot_row = CHUNK // 2, 0
    else:
        cum = (ci <= ri).astype(BF16)
        att_mask = ci <= ri
        ref_row, tot_row = CHUNK // 2 - 1, CHUNK - 1
    head_of_lane = lax.broadcasted_iota(jnp.int32, (1, D_GLA_K), 1) // HEAD_K
    head_masks = [(head_of_lane == h).astype(F32) for h in range(GLA_HEADS)]

    def body(step, carry):
        c = (n_chunks - 1 - step) if reverse else step
        r0 = pl.multiple_of(c * CHUNK, CHUNK)
        q = q_ref[pl.ds(r0, CHUNK), q0:q0 + D_GLA_K] * (HEAD_K ** -0.5)
        k = k_ref[pl.ds(r0, CHUNK), k0:k0 + D_GLA_K]
        v = v_ref[pl.ds(r0, CHUNK), v0:v0 + D_GLA_V]
        la = la_ref[pl.ds(r0, CHUNK), :]
        la_hi = la.astype(BF16)
        la_lo = (la - la_hi.astype(F32)).astype(BF16)
        b = _dot(cum, la_hi) + _dot(cum, la_lo)
        b_ref = b[ref_row:ref_row + 1, :]
        b_tot = b[tot_row:tot_row + 1, :]
        q_in = q * jnp.exp(b - b_ref)
        k_in = (k * jnp.exp(b_ref - b)).astype(BF16)
        k_dec = k * jnp.exp(b_tot - b)
        q_dec = q * jnp.exp(b)
        state = st_ref[...]
        state_b = state.astype(BF16)
        kv_t = jnp.zeros((HEAD_V, D_GLA_K), F32)
        outs = []
        for h in range(GLA_HEADS):
            m = head_masks[h]
            att = _dot_nt((q_in * m).astype(BF16), k_in)
            att = jnp.where(att_mask, att, 0.0)
            v_h = v[:, h * HEAD_V:(h + 1) * HEAD_V]
            o_h = _dot(att.astype(BF16), v_h.astype(BF16))
            o_h = o_h + _dot_nt((q_dec * m).astype(BF16), state_b)
            outs.append(o_h)
            kv_t = kv_t + _dot(v_h.T.astype(BF16), (k_dec * m).astype(BF16))
        st_ref[...] = state * jnp.exp(b_tot) + kv_t
        o_ref[pl.ds(r0, CHUNK), :] = jnp.concatenate(outs, axis=1)
        return carry

    lax.fori_loop(0, n_chunks, body, 0)


def _gla_bwd_kernel(x_ref, wqkv_ref, wlr_ref, w2_ref, b2_ref, o_ref, p_ref, la_ref, st_ref):
    @pl.when(pl.program_id(1) == 0)
    def _():
        st_ref[...] = jnp.zeros(st_ref.shape, F32)

    xb = x_ref[0].astype(BF16)
    p_ref[...] = _dot(xb, wqkv_ref[...])
    la_ref[...] = _log_decay(xb, wlr_ref, w2_ref, b2_ref)
    _gla_tile(p_ref, p_ref, p_ref, la_ref, o_ref.at[0], st_ref,
              q0=0, k0=D_GLA_K, v0=2 * D_GLA_K, reverse=True)


def _gla_bwd_call(x, w_qkv, w_lr, w2_b, b2_b):
    bsz, seq, _ = x.shape
    nt = seq // TIME_TILE
    rev = lambda b, i: (b, nt - 1 - i, 0)
    return pl.pallas_call(
        _gla_bwd_kernel,
        grid=(bsz, nt),
        in_specs=[
            pl.BlockSpec((1, TIME_TILE, D_MODEL), rev),
            _const_spec((D_MODEL, 2 * D_GLA_K + D_GLA_V)),
            _const_spec((D_MODEL, LR_PAD)),
            _const_spec((LR_PAD, D_GLA_K)),
            _const_spec((1, D_GLA_K)),
        ],
        out_specs=pl.BlockSpec((1, TIME_TILE, D_GLA_V), rev),
        out_shape=jax.ShapeDtypeStruct((bsz, seq, D_GLA_V), F32),
        scratch_shapes=[
            pltpu.VMEM((TIME_TILE, 2 * D_GLA_K + D_GLA_V), F32),
            pltpu.VMEM((TIME_TILE, D_GLA_K), F32),
            pltpu.VMEM((HEAD_V, D_GLA_K), F32),
        ],
        compiler_params=pltpu.CompilerParams(
            dimension_semantics=("arbitrary", "arbitrary"), vmem_limit_bytes=VMEM_LIMIT_BYTES),
        name="gla_bwd",
    )(x, w_qkv, w_lr, w2_b, b2_b)


def _mixer_kernel(x_ref, xprev_ref, xnext_ref, ob_ref, wp_ref, wlr_ref, w2_ref, b2_ref, wm_ref,
                  convw_ref, wco_ref, gn_ref, wgo_ref, wout_ref, g_ref, b_ref,
                  o_ref, p_ref, la_ref, of_ref, st_ref):
    i = pl.program_id(1)
    nt = pl.num_programs(1)

    @pl.when(i == 0)
    def _():
        st_ref[...] = jnp.zeros(st_ref.shape, F32)

    x = x_ref[0]
    xb = x.astype(BF16)
    p_ref[...] = _dot(xb, wp_ref[...])
    la_ref[...] = _log_decay(xb, wlr_ref, w2_ref, b2_ref)

    _gla_tile(p_ref, p_ref, p_ref, la_ref, of_ref, st_ref, q0=_Q0, k0=_K0, v0=_V0, reverse=False)
    gn = gn_ref[...]
    normed = []
    for h in range(GLA_HEADS):
        sl = slice(h * HEAD_V, (h + 1) * HEAD_V)
        o_h = of_ref[:, sl] + ob_ref[0, :, sl]
        ms = jnp.mean(o_h * o_h, axis=-1, keepdims=True)
        normed.append(o_h * lax.rsqrt(ms + RMS_EPS) * gn)
    gate = p_ref[:, _G0:_G0 + D_GLA_V]
    o_gla = jnp.concatenate(normed, axis=1) * (gate * jax.nn.sigmoid(gate))
    y_gla = _dot(o_gla.astype(BF16), wgo_ref[...])

    u = p_ref[:, _GC0:_GC0 + D_CONV] * p_ref[:, _H0:_H0 + D_CONV]
    xh = jnp.concatenate([xprev_ref[0], xnext_ref[0]], axis=0).astype(BF16)
    ph = _dot(xh, wp_ref[:, 0:2 * D_CONV])
    uh = ph[:, _GC0:_GC0 + D_CONV] * ph[:, _H0:_H0 + D_CONV]
    u_before = jnp.where(i > 0, uh[SUBLANES - 1:SUBLANES, :], 0.0)
    u_after = jnp.where(i < nt - 1, uh[SUBLANES:SUBLANES + 1, :], 0.0)
    rows = u.shape[0]
    row = lax.broadcasted_iota(jnp.int32, (rows, 1), 0)
    u_m1 = jnp.where(row == 0, u_before, pltpu.roll(u, 1, axis=0))
    u_p1 = jnp.where(row == rows - 1, u_after, pltpu.roll(u, rows - 1, axis=0))
    cw = convw_ref[...]
    conv = u_m1 * cw[0:1, :] + u * cw[1:2, :] + u_p1 * cw[2:3, :]
    y_conv = _dot((p_ref[:, _GB0:_GB0 + D_CONV] * conv).astype(BF16), wco_ref[...])

    mg = _dot(xb, wm_ref[...])
    merged = (jax.nn.sigmoid(mg[:, :D_MODEL]) * y_conv
              + jax.nn.sigmoid(mg[:, D_MODEL:]) * y_gla)
    y = _dot(merged.astype(BF16), wout_ref[...])
    o_ref[0] = _residual_layer_norm(x, y, g_ref[...], b_ref[...])


def _mixer_call(x, o_b, w):
    bsz, seq, _ = x.shape
    nt = seq // TIME_TILE
    per8 = TIME_TILE // SUBLANES
    last8 = seq // SUBLANES - 1
    tile = lambda b, i: (b, i, 0)
    return pl.pallas_call(
        _mixer_kernel,
        grid=(bsz, nt),
        in_specs=[
            pl.BlockSpec((1, TIME_TILE, D_MODEL), tile),
            pl.BlockSpec((1, SUBLANES, D_MODEL), lambda b, i: (b, jnp.maximum(i * per8 - 1, 0), 0)),
            pl.BlockSpec((1, SUBLANES, D_MODEL), lambda b, i: (b, jnp.minimum((i + 1) * per8, last8), 0)),
            pl.BlockSpec((1, TIME_TILE, D_GLA_V), tile),
            _const_spec((D_MODEL, D_PROJ)),
            _const_spec((D_MODEL, LR_PAD)),
            _const_spec((LR_PAD, D_GLA_K)),
            _const_spec((1, D_GLA_K)),
            _const_spec((D_MODEL, 2 * D_MODEL)),
            _const_spec((3, D_CONV)),
            _const_spec((D_CONV, D_MODEL)),
            _const_spec((1, HEAD_V)),
            _const_spec((D_GLA_V, D_MODEL)),
            _const_spec((D_MODEL, D_MODEL)),
            _const_spec((1, D_MODEL)),
            _const_spec((1, D_MODEL)),
        ],
        out_specs=pl.BlockSpec((1, TIME_TILE, D_MODEL), tile),
        out_shape=jax.ShapeDtypeStruct((bsz, seq, D_MODEL), F32),
        scratch_shapes=[
            pltpu.VMEM((TIME_TILE, D_PROJ), F32),
            pltpu.VMEM((TIME_TILE, D_GLA_K), F32),
            pltpu.VMEM((TIME_TILE, D_GLA_V), F32),
            pltpu.VMEM((HEAD_V, D_GLA_K), F32),
        ],
        compiler_params=pltpu.CompilerParams(
            dimension_semantics=("arbitrary", "arbitrary"), vmem_limit_bytes=VMEM_LIMIT_BYTES),
        name="mixer_ln",
    )(x, x, x, o_b, w["w_proj"], w["w_lr"], w["w2_f"], w["b2_f"], w["w_merge"], w["conv_w"],
      w["w_conv_out"], w["gla_norm_g"], w["w_gla_out"], w["w_mix_out"], w["ln_g1"], w["ln_b1"])


def _xattn_kernel(x_ref, mem_ref, wq_ref, wkv_ref, wo_ref, g_ref, b_ref, o_ref, kv_ref):
    @pl.when(pl.program_id(1) == 0)
    def _():
        kv_ref[...] = _dot(mem_ref[0].astype(BF16), wkv_ref[...]).astype(BF16)

    x = x_ref[0]
    q = _dot(x.astype(BF16), wq_ref[...]).astype(BF16)
    heads = []
    for h in range(XA_HEADS):
        sl = slice(h * XA_HEAD_DIM, (h + 1) * XA_HEAD_DIM)
        k_h = kv_ref[:, sl]
        v_h = kv_ref[:, D_MODEL + h * XA_HEAD_DIM:D_MODEL + (h + 1) * XA_HEAD_DIM]
        s = _dot_nt(q[:, sl], k_h) * (XA_HEAD_DIM ** -0.5)
        e = jnp.exp(s - jnp.max(s, axis=-1, keepdims=True))
        p = e / jnp.sum(e, axis=-1, keepdims=True)
        heads.append(_dot(p.astype(BF16), v_h))
    o = jnp.concatenate(heads, axis=1)
    y = _dot(o.astype(BF16), wo_ref[...])
    o_ref[0] = _residual_layer_norm(x, y, g_ref[...], b_ref[...])


def _xattn_call(x, mem, w_q, w_kv, w_o, ln_g, ln_b):
    bsz, seq, _ = x.shape
    n_mem = mem.shape[1]
    tile = lambda b, i: (b, i, 0)
    return pl.pallas_call(
        _xattn_kernel,
        grid=(bsz, seq // TIME_TILE),
        in_specs=[
            pl.BlockSpec((1, TIME_TILE, D_MODEL), tile),
            pl.BlockSpec((1, n_mem, D_MODEL), lambda b, i: (b, 0, 0)),
            _const_spec((D_MODEL, D_MODEL)),
            _const_spec((D_MODEL, 2 * D_MODEL)),
            _const_spec((D_MODEL, D_MODEL)),
            _const_spec((1, D_MODEL)),
            _const_spec((1, D_MODEL)),
        ],
        out_specs=pl.BlockSpec((1, TIME_TILE, D_MODEL), tile),
        out_shape=jax.ShapeDtypeStruct((bsz, seq, D_MODEL), F32),
        scratch_shapes=[pltpu.VMEM((n_mem, 2 * D_MODEL), BF16)],
        compiler_params=pltpu.CompilerParams(
            dimension_semantics=("arbitrary", "arbitrary"), vmem_limit_bytes=VMEM_LIMIT_BYTES),
        name="xattn_ln",
    )(x, mem, w_q, w_kv, w_o, ln_g, ln_b)


def _pack_layer(l, ffn1_w_gu, ffn1_w_down, w_mix_in, conv_w, w_conv_out, gla_gate_w2, gla_gate_b,
                gla_norm_g, w_gla_out, w_mix_out, xa_w_q, xa_w_kv, xa_w_o, ffn2_w_gu, ffn2_w_down,
                ln_g, ln_b):
    w_in = w_mix_in[l]
    o = 0
    cols = {}
    for name, width in (("h", D_CONV), ("gb", D_CONV), ("gc", D_CONV), ("q", D_GLA_K), ("k", D_GLA_K),
                        ("v", D_GLA_V), ("g", D_GLA_V), ("lr_f", GATE_RANK), ("lr_b", GATE_RANK),
                        ("m_conv", D_MODEL), ("m_gla", D_MODEL)):
        cols[name] = w_in[:, o:o + width]
        o += width
    cat = lambda names: jnp.concatenate([cols[n] for n in names], axis=1).astype(BF16)
    w_lr = jnp.pad(jnp.concatenate([cols["lr_f"], cols["lr_b"]], axis=1),
                   ((0, 0), (0, LR_PAD - 2 * GATE_RANK))).astype(BF16)
    w2 = gla_gate_w2[l]
    w2_f = jnp.pad(w2[0], ((0, LR_PAD - GATE_RANK), (0, 0))).astype(BF16)
    w2_b = jnp.pad(w2[1], ((GATE_RANK, LR_PAD - 2 * GATE_RANK), (0, 0))).astype(BF16)
    row = lambda a: a.reshape(1, -1)
    return dict(
        ffn1_w_gu=ffn1_w_gu[l].astype(BF16), ffn1_w_down=ffn1_w_down[l].astype(BF16),
        ffn2_w_gu=ffn2_w_gu[l].astype(BF16), ffn2_w_down=ffn2_w_down[l].astype(BF16),
        w_proj=cat(("h", "gc", "gb", "q", "k", "v", "g")), w_qkv=cat(("q", "k", "v")),
        w_lr=w_lr, w2_f=w2_f, w2_b=w2_b,
        b2_f=row(gla_gate_b[l, 0]), b2_b=row(gla_gate_b[l, 1]),
        w_merge=cat(("m_conv", "m_gla")), conv_w=conv_w[l],
        w_conv_out=w_conv_out[l].astype(BF16), gla_norm_g=row(gla_norm_g[l]),
        w_gla_out=w_gla_out[l].astype(BF16), w_mix_out=w_mix_out[l].astype(BF16),
        xa_w_q=xa_w_q[l].astype(BF16), xa_w_kv=xa_w_kv[l].astype(BF16), xa_w_o=xa_w_o[l].astype(BF16),
        ln_g0=row(ln_g[l, 0]), ln_b0=row(ln_b[l, 0]), ln_g1=row(ln_g[l, 1]), ln_b1=row(ln_b[l, 1]),
        ln_g2=row(ln_g[l, 2]), ln_b2=row(ln_b[l, 2]), ln_g3=row(ln_g[l, 3]), ln_b3=row(ln_b[l, 3]),
    )


def _trunk(x, mem, layers):
    bsz, seq, d = x.shape
    for w in layers:
        x = _ffn_call(x.reshape(bsz * seq, d), w["ffn1_w_gu"], w["ffn1_w_down"],
                      w["ln_g0"], w["ln_b0"]).reshape(bsz, seq, d)
        o_b = _gla_bwd_call(x, w["w_qkv"], w["w_lr"], w["w2_b"], w["b2_b"])
        x = _mixer_call(x, o_b, w)
        x = _xattn_call(x, mem, w["xa_w_q"], w["xa_w_kv"], w["xa_w_o"], w["ln_g2"], w["ln_b2"])
        x = _ffn_call(x.reshape(bsz * seq, d), w["ffn2_w_gu"], w["ffn2_w_down"],
                      w["ln_g3"], w["ln_b3"]).reshape(bsz, seq, d)
    return x


def kernel(x_prompt, x_sample, mem_prompt, mem_sample, ffn1_w_gu, ffn1_w_down, w_mix_in, conv_w, w_conv_out, gla_gate_w2, gla_gate_b, gla_norm_g, w_gla_out, w_mix_out, xa_w_q, xa_w_kv, xa_w_o, ffn2_w_gu, ffn2_w_down, ln_g, ln_b):
    layers = [_pack_layer(l, ffn1_w_gu, ffn1_w_down, w_mix_in, conv_w, w_conv_out, gla_gate_w2,
                          gla_gate_b, gla_norm_g, w_gla_out, w_mix_out, xa_w_q, xa_w_kv, xa_w_o,
                          ffn2_w_gu, ffn2_w_down, ln_g, ln_b) for l in range(DEPTH)]
    return (_trunk(x_prompt, mem_prompt, layers), _trunk(x_sample, mem_sample, layers))
```

```python
import functools

import jax
import jax.numpy as jnp
from jax import lax
from jax.experimental import pallas as pl
from jax.experimental.pallas import tpu as pltpu

F32 = jnp.float32
BF16 = jnp.bfloat16

D_MODEL = 1024
DEPTH = 4
D_FF = 2816
D_CONV = D_MODEL // 2
GLA_HEADS = 4
D_GLA_V = D_MODEL // 2
HEAD_V = D_GLA_V // GLA_HEADS
D_GLA_K = D_GLA_V // 2
HEAD_K = D_GLA_K // GLA_HEADS
GATE_RANK = 16
GATE_TEMP = 16.0
CHUNK = 64
XA_HEADS = 4
XA_HEAD_DIM = D_MODEL // XA_HEADS
LN_EPS = 1e-5
RMS_EPS = 1e-6
DN_ALPHA = (2 * DEPTH) ** 0.25

SUBLANES = 8
LANES = 128
VMEM_LIMIT_BYTES = 56 * 1024 * 1024

TOKEN_TILE = 512
TIME_TILE = 512
FF_CHUNK = 1408
LR_PAD = LANES

_H0, _GC0, _GB0 = 0, D_CONV, 2 * D_CONV
_Q0 = 3 * D_CONV
_K0 = _Q0 + D_GLA_K
_V0 = _K0 + D_GLA_K
_G0 = _V0 + D_GLA_V
D_PROJ = _G0 + D_GLA_V


def _dot(a, b):
    return jnp.dot(a, b, preferred_element_type=F32)


def _dot_nt(a, b):
    return lax.dot_general(a, b, (((1,), (1,)), ((), ())), preferred_element_type=F32)


def _residual_layer_norm(x, y, g, b):
    z = DN_ALPHA * x + y
    mu = jnp.mean(z, axis=-1, keepdims=True)
    zc = z - mu
    var = jnp.mean(zc * zc, axis=-1, keepdims=True)
    return zc * lax.rsqrt(var + LN_EPS) * g + b


def _const_spec(shape):
    n = len(shape)
    return pl.BlockSpec(shape, lambda *_: (0,) * n)


def _ffn_kernel(x_ref, wgu_ref, wd_ref, g_ref, b_ref, o_ref):
    x = x_ref[...]
    xb = x.astype(BF16)
    acc = jnp.zeros(x.shape, F32)
    for c0 in range(0, D_FF, FF_CHUNK):
        gate = _dot(xb, wgu_ref[:, c0:c0 + FF_CHUNK])
        up = _dot(xb, wgu_ref[:, D_FF + c0:D_FF + c0 + FF_CHUNK])
        act = (gate * jax.nn.sigmoid(gate)) * up
        acc = acc + _dot(act.astype(BF16), wd_ref[c0:c0 + FF_CHUNK, :])
    o_ref[...] = _residual_layer_norm(x, 0.5 * acc, g_ref[...], b_ref[...])


def _ffn_call(x2d, w_gu, w_down, ln_g, ln_b):
    n = x2d.shape[0]
    assert n % TOKEN_TILE == 0
    return pl.pallas_call(
        _ffn_kernel,
        grid=(n // TOKEN_TILE,),
        in_specs=[
            pl.BlockSpec((TOKEN_TILE, D_MODEL), lambda i: (i, 0)),
            _const_spec((D_MODEL, 2 * D_FF)),
            _const_spec((D_FF, D_MODEL)),
            _const_spec((1, D_MODEL)),
            _const_spec((1, D_MODEL)),
        ],
        out_specs=pl.BlockSpec((TOKEN_TILE, D_MODEL), lambda i: (i, 0)),
        out_shape=jax.ShapeDtypeStruct((n, D_MODEL), F32),
        compiler_params=pltpu.CompilerParams(
            dimension_semantics=("arbitrary",), vmem_limit_bytes=VMEM_LIMIT_BYTES),
        name="ffn_ln",
    )(x2d, w_gu, w_down, ln_g, ln_b)


def _log_decay(xb, wlr_ref, w2_ref, b2_ref):
    lr = _dot(xb, wlr_ref[...])
    z = _dot(lr.astype(BF16), w2_ref[...]) + b2_ref[...]
    return jax.nn.log_sigmoid(z) / GATE_TEMP


def _gla_tile(q_ref, k_ref, v_ref, la_ref, o_ref, st_ref, *, q0, k0, v0, reverse):
    rows = q_ref.shape[0]
    n_chunks = rows // CHUNK
    ri = lax.broadcasted_iota(jnp.int32, (CHUNK, CHUNK), 0)
    ci = lax.broadcasted_iota(jnp.int32, (CHUNK, CHUNK), 1)
    stacked = GLA_HEADS * CHUNK
    sr = lax.broadcasted_iota(jnp.int32, (stacked, stacked), 0)
    sc = lax.broadcasted_iota(jnp.int32, (stacked, stacked), 1)
    same_head = (sr // CHUNK) == (sc // CHUNK)
    if reverse:
        cum = (ci >= ri).astype(BF16)
        att_mask = same_head & ((sc % CHUNK) > (sr % CHUNK))
        ref_row, tot_row = CHUNK // 2, 0
    else:
        cum = (ci <= ri).astype(BF16)
        att_mask = same_head & ((sc % CHUNK) <= (sr % CHUNK))
        ref_row, tot_row = CHUNK // 2 - 1, CHUNK - 1
    head_of_lane = lax.broadcasted_iota(jnp.int32, (1, D_GLA_K), 1) // HEAD_K
    head_masks = [(head_of_lane == h).astype(F32) for h in range(GLA_HEADS)]

    def stack_heads(a):
        return jnp.concatenate([(a * m).astype(BF16) for m in head_masks], axis=0)

    for step in range(n_chunks):
        r0 = (n_chunks - 1 - step if reverse else step) * CHUNK
        q = q_ref[r0:r0 + CHUNK, q0:q0 + D_GLA_K] * (HEAD_K ** -0.5)
        k = k_ref[r0:r0 + CHUNK, k0:k0 + D_GLA_K]
        v = v_ref[r0:r0 + CHUNK, v0:v0 + D_GLA_V]
        la = la_ref[r0:r0 + CHUNK, :]
        la_hi = la.astype(BF16)
        la_lo = (la - la_hi.astype(F32)).astype(BF16)
        b = _dot(cum, la_hi) + _dot(cum, la_lo)
        b_ref = b[ref_row:ref_row + 1, :]
        b_tot = b[tot_row:tot_row + 1, :]
        q_in = q * jnp.exp(b - b_ref)
        k_in = (k * jnp.exp(b_ref - b)).astype(BF16)
        k_dec = k * jnp.exp(b_tot - b)
        q_dec = q * jnp.exp(b)
        att = _dot_nt(stack_heads(q_in), jnp.concatenate([k_in] * GLA_HEADS, axis=0))
        att = jnp.where(att_mask, att, 0.0).astype(BF16)
        v_st = jnp.concatenate([v[:, h * HEAD_V:(h + 1) * HEAD_V] for h in range(GLA_HEADS)], axis=0)
        state = st_ref[...]
        o_st = _dot(att, v_st.astype(BF16)) + _dot_nt(stack_heads(q_dec), state.astype(BF16))
        kv_t = _dot(v_st.T.astype(BF16), stack_heads(k_dec))
        st_ref[...] = state * jnp.exp(b_tot) + kv_t
        o_ref[r0:r0 + CHUNK, :] = jnp.concatenate(
            [o_st[h * CHUNK:(h + 1) * CHUNK, :] for h in range(GLA_HEADS)], axis=1)


def _gla_bwd_kernel(x_ref, wqkv_ref, wlr_ref, w2_ref, b2_ref, o_ref, p_ref, la_ref, st_ref):
    @pl.when(pl.program_id(1) == 0)
    def _():
        st_ref[...] = jnp.zeros(st_ref.shape, F32)

    xb = x_ref[0].astype(BF16)
    p_ref[...] = _dot(xb, wqkv_ref[...])
    la_ref[...] = _log_decay(xb, wlr_ref, w2_ref, b2_ref)
    _gla_tile(p_ref, p_ref, p_ref, la_ref, o_ref.at[0], st_ref,
              q0=0, k0=D_GLA_K, v0=2 * D_GLA_K, reverse=True)


def _gla_bwd_call(x, w_qkv, w_lr, w2_b, b2_b):
    bsz, seq, _ = x.shape
    nt = seq // TIME_TILE
    rev = lambda b, i: (b, nt - 1 - i, 0)
    return pl.pallas_call(
        _gla_bwd_kernel,
        grid=(bsz, nt),
        in_specs=[
            pl.BlockSpec((1, TIME_TILE, D_MODEL), rev),
            _const_spec((D_MODEL, 2 * D_GLA_K + D_GLA_V)),
            _const_spec((D_MODEL, LR_PAD)),
            _const_spec((LR_PAD, D_GLA_K)),
            _const_spec((1, D_GLA_K)),
        ],
        out_specs=pl.BlockSpec((1, TIME_TILE, D_GLA_V), rev),
        out_shape=jax.ShapeDtypeStruct((bsz, seq, D_GLA_V), F32),
        scratch_shapes=[
            pltpu.VMEM((TIME_TILE, 2 * D_GLA_K + D_GLA_V), F32),
            pltpu.VMEM((TIME_TILE, D_GLA_K), F32),
            pltpu.VMEM((HEAD_V, D_GLA_K), F32),
        ],
        compiler_params=pltpu.CompilerParams(
            dimension_semantics=("arbitrary", "arbitrary"), vmem_limit_bytes=VMEM_LIMIT_BYTES),
        name="gla_bwd",
    )(x, w_qkv, w_lr, w2_b, b2_b)


def _mixer_kernel(x_ref, xprev_ref, xnext_ref, ob_ref, wp_ref, wlr_ref, w2_ref, b2_ref, wm_ref,
                  convw_ref, wco_ref, gn_ref, wgo_ref, wout_ref, g_ref, b_ref,
                  o_ref, p_ref, la_ref, of_ref, st_ref):
    i = pl.program_id(1)
    nt = pl.num_programs(1)

    @pl.when(i == 0)
    def _():
        st_ref[...] = jnp.zeros(st_ref.shape, F32)

    x = x_ref[0]
    xb = x.astype(BF16)
    p_ref[...] = _dot(xb, wp_ref[...])
    la_ref[...] = _log_decay(xb, wlr_ref, w2_ref, b2_ref)

    _gla_tile(p_ref, p_ref, p_ref, la_ref, of_ref, st_ref, q0=_Q0, k0=_K0, v0=_V0, reverse=False)
    gn = gn_ref[...]
    normed = []
    for h in range(GLA_HEADS):
        sl = slice(h * HEAD_V, (h + 1) * HEAD_V)
        o_h = of_ref[:, sl] + ob_ref[0, :, sl]
        ms = jnp.mean(o_h * o_h, axis=-1, keepdims=True)
        normed.append(o_h * lax.rsqrt(ms + RMS_EPS) * gn)
    gate = p_ref[:, _G0:_G0 + D_GLA_V]
    o_gla = jnp.concatenate(normed, axis=1) * (gate * jax.nn.sigmoid(gate))
    y_gla = _dot(o_gla.astype(BF16), wgo_ref[...])

    u = p_ref[:, _GC0:_GC0 + D_CONV] * p_ref[:, _H0:_H0 + D_CONV]
    xh = jnp.concatenate([xprev_ref[0], xnext_ref[0]], axis=0).astype(BF16)
    ph = _dot(xh, wp_ref[:, 0:2 * D_CONV])
    uh = ph[:, _GC0:_GC0 + D_CONV] * ph[:, _H0:_H0 + D_CONV]
    u_before = jnp.where(i > 0, uh[SUBLANES - 1:SUBLANES, :], 0.0)
    u_after = jnp.where(i < nt - 1, uh[SUBLANES:SUBLANES + 1, :], 0.0)
    rows = u.shape[0]
    row = lax.broadcasted_iota(jnp.int32, (rows, 1), 0)
    u_m1 = jnp.where(row == 0, u_before, pltpu.roll(u, 1, axis=0))
    u_p1 = jnp.where(row == rows - 1, u_after, pltpu.roll(u, rows - 1, axis=0))
    cw = convw_ref[...]
    conv = u_m1 * cw[0:1, :] + u * cw[1:2, :] + u_p1 * cw[2:3, :]
    y_conv = _dot((p_ref[:, _GB0:_GB0 + D_CONV] * conv).astype(BF16), wco_ref[...])

    mg = _dot(xb, wm_ref[...])
    merged = (jax.nn.sigmoid(mg[:, :D_MODEL]) * y_conv
              + jax.nn.sigmoid(mg[:, D_MODEL:]) * y_gla)
    y = _dot(merged.astype(BF16), wout_ref[...])
    o_ref[0] = _residual_layer_norm(x, y, g_ref[...], b_ref[...])


def _mixer_call(x, o_b, w):
    bsz, seq, _ = x.shape
    nt = seq // TIME_TILE
    per8 = TIME_TILE // SUBLANES
    last8 = seq // SUBLANES - 1
    tile = lambda b, i: (b, i, 0)
    return pl.pallas_call(
        _mixer_kernel,
        grid=(bsz, nt),
        in_specs=[
            pl.BlockSpec((1, TIME_TILE, D_MODEL), tile),
            pl.BlockSpec((1, SUBLANES, D_MODEL), lambda b, i: (b, jnp.maximum(i * per8 - 1, 0), 0)),
            pl.BlockSpec((1, SUBLANES, D_MODEL), lambda b, i: (b, jnp.minimum((i + 1) * per8, last8), 0)),
            pl.BlockSpec((1, TIME_TILE, D_GLA_V), tile),
            _const_spec((D_MODEL, D_PROJ)),
            _const_spec((D_MODEL, LR_PAD)),
            _const_spec((LR_PAD, D_GLA_K)),
            _const_spec((1, D_GLA_K)),
            _const_spec((D_MODEL, 2 * D_MODEL)),
            _const_spec((3, D_CONV)),
            _const_spec((D_CONV, D_MODEL)),
            _const_spec((1, HEAD_V)),
            _const_spec((D_GLA_V, D_MODEL)),
            _const_spec((D_MODEL, D_MODEL)),
            _const_spec((1, D_MODEL)),
            _const_spec((1, D_MODEL)),
        ],
        out_specs=pl.BlockSpec((1, TIME_TILE, D_MODEL), tile),
        out_shape=jax.ShapeDtypeStruct((bsz, seq, D_MODEL), F32),
        scratch_shapes=[
            pltpu.VMEM((TIME_TILE, D_PROJ), F32),
            pltpu.VMEM((TIME_TILE, D_GLA_K), F32),
            pltpu.VMEM((TIME_TILE, D_GLA_V), F32),
            pltpu.VMEM((HEAD_V, D_GLA_K), F32),
        ],
        compiler_params=pltpu.CompilerParams(
            dimension_semantics=("arbitrary", "arbitrary"), vmem_limit_bytes=VMEM_LIMIT_BYTES),
        name="mixer_ln",
    )(x, x, x, o_b, w["w_proj"], w["w_lr"], w["w2_f"], w["b2_f"], w["w_merge"], w["conv_w"],
      w["w_conv_out"], w["gla_norm_g"], w["w_gla_out"], w["w_mix_out"], w["ln_g1"], w["ln_b1"])


def _xattn_kernel(x_ref, mem_ref, wq_ref, wkv_ref, wo_ref, g_ref, b_ref, o_ref, kv_ref):
    @pl.when(pl.program_id(1) == 0)
    def _():
        kv_ref[...] = _dot(mem_ref[0].astype(BF16), wkv_ref[...]).astype(BF16)

    x = x_ref[0]
    q = _dot(x.astype(BF16), wq_ref[...]).astype(BF16)
    heads = []
    for h in range(XA_HEADS):
        sl = slice(h * XA_HEAD_DIM, (h + 1) * XA_HEAD_DIM)
        k_h = kv_ref[:, sl]
        v_h = kv_ref[:, D_MODEL + h * XA_HEAD_DIM:D_MODEL + (h + 1) * XA_HEAD_DIM]
        s = _dot_nt(q[:, sl], k_h) * (XA_HEAD_DIM ** -0.5)
        e = jnp.exp(s - jnp.max(s, axis=-1, keepdims=True))
        p = e / jnp.sum(e, axis=-1, keepdims=True)
        heads.append(_dot(p.astype(BF16), v_h))
    o = jnp.concatenate(heads, axis=1)
    y = _dot(o.astype(BF16), wo_ref[...])
    o_ref[0] = _residual_layer_norm(x, y, g_ref[...], b_ref[...])


def _xattn_call(x, mem, w_q, w_kv, w_o, ln_g, ln_b):
    bsz, seq, _ = x.shape
    n_mem = mem.shape[1]
    tile = lambda b, i: (b, i, 0)
    return pl.pallas_call(
        _xattn_kernel,
        grid=(bsz, seq // TIME_TILE),
        in_specs=[
            pl.BlockSpec((1, TIME_TILE, D_MODEL), tile),
            pl.BlockSpec((1, n_mem, D_MODEL), lambda b, i: (b, 0, 0)),
            _const_spec((D_MODEL, D_MODEL)),
            _const_spec((D_MODEL, 2 * D_MODEL)),
            _const_spec((D_MODEL, D_MODEL)),
            _const_spec((1, D_MODEL)),
            _const_spec((1, D_MODEL)),
        ],
        out_specs=pl.BlockSpec((1, TIME_TILE, D_MODEL), tile),
        out_shape=jax.ShapeDtypeStruct((bsz, seq, D_MODEL), F32),
        scratch_shapes=[pltpu.VMEM((n_mem, 2 * D_MODEL), BF16)],
        compiler_params=pltpu.CompilerParams(
            dimension_semantics=("arbitrary", "arbitrary"), vmem_limit_bytes=VMEM_LIMIT_BYTES),
        name="xattn_ln",
    )(x, mem, w_q, w_kv, w_o, ln_g, ln_b)


def _pack_layer(l, ffn1_w_gu, ffn1_w_down, w_mix_in, conv_w, w_conv_out, gla_gate_w2, gla_gate_b,
                gla_norm_g, w_gla_out, w_mix_out, xa_w_q, xa_w_kv, xa_w_o, ffn2_w_gu, ffn2_w_down,
                ln_g, ln_b):
    w_in = w_mix_in[l]
    o = 0
    cols = {}
    for name, width in (("h", D_CONV), ("gb", D_CONV), ("gc", D_CONV), ("q", D_GLA_K), ("k", D_GLA_K),
                        ("v", D_GLA_V), ("g", D_GLA_V), ("lr_f", GATE_RANK), ("lr_b", GATE_RANK),
                        ("m_conv", D_MODEL), ("m_gla", D_MODEL)):
        cols[name] = w_in[:, o:o + width]
        o += width
    cat = lambda names: jnp.concatenate([cols[n] for n in names], axis=1).astype(BF16)
    w_lr = jnp.pad(jnp.concatenate([cols["lr_f"], cols["lr_b"]], axis=1),
                   ((0, 0), (0, LR_PAD - 2 * GATE_RANK))).astype(BF16)
    w2 = gla_gate_w2[l]
    w2_f = jnp.pad(w2[0], ((0, LR_PAD - GATE_RANK), (0, 0))).astype(BF16)
    w2_b = jnp.pad(w2[1], ((GATE_RANK, LR_PAD - 2 * GATE_RANK), (0, 0))).astype(BF16)
    row = lambda a: a.reshape(1, -1)
    return dict(
        ffn1_w_gu=ffn1_w_gu[l].astype(BF16), ffn1_w_down=ffn1_w_down[l].astype(BF16),
        ffn2_w_gu=ffn2_w_gu[l].astype(BF16), ffn2_w_down=ffn2_w_down[l].astype(BF16),
        w_proj=cat(("h", "gc", "gb", "q", "k", "v", "g")), w_qkv=cat(("q", "k", "v")),
        w_lr=w_lr, w2_f=w2_f, w2_b=w2_b,
        b2_f=row(gla_gate_b[l, 0]), b2_b=row(gla_gate_b[l, 1]),
        w_merge=cat(("m_conv", "m_gla")), conv_w=conv_w[l],
        w_conv_out=w_conv_out[l].astype(BF16), gla_norm_g=row(gla_norm_g[l]),
        w_gla_out=w_gla_out[l].astype(BF16), w_mix_out=w_mix_out[l].astype(BF16),
        xa_w_q=xa_w_q[l].astype(BF16), xa_w_kv=xa_w_kv[l].astype(BF16), xa_w_o=xa_w_o[l].astype(BF16),
        ln_g0=row(ln_g[l, 0]), ln_b0=row(ln_b[l, 0]), ln_g1=row(ln_g[l, 1]), ln_b1=row(ln_b[l, 1]),
        ln_g2=row(ln_g[l, 2]), ln_b2=row(ln_b[l, 2]), ln_g3=row(ln_g[l, 3]), ln_b3=row(ln_b[l, 3]),
    )


def _trunk(x, mem, layers):
    bsz, seq, d = x.shape
    for w in layers:
        x = _ffn_call(x.reshape(bsz * seq, d), w["ffn1_w_gu"], w["ffn1_w_down"],
                      w["ln_g0"], w["ln_b0"]).reshape(bsz, seq, d)
        o_b = _gla_bwd_call(x, w["w_qkv"], w["w_lr"], w["w2_b"], w["b2_b"])
        x = _mixer_call(x, o_b, w)
        x = _xattn_call(x, mem, w["xa_w_q"], w["xa_w_kv"], w["xa_w_o"], w["ln_g2"], w["ln_b2"])
        x = _ffn_call(x.reshape(bsz * seq, d), w["ffn2_w_gu"], w["ffn2_w_down"],
                      w["ln_g3"], w["ln_b3"]).reshape(bsz, seq, d)
    return x


def kernel(x_prompt, x_sample, mem_prompt, mem_sample, ffn1_w_gu, ffn1_w_down, w_mix_in, conv_w, w_conv_out, gla_gate_w2, gla_gate_b, gla_norm_g, w_gla_out, w_mix_out, xa_w_q, xa_w_kv, xa_w_o, ffn2_w_gu, ffn2_w_down, ln_g, ln_b):
    layers = [_pack_layer(l, ffn1_w_gu, ffn1_w_down, w_mix_in, conv_w, w_conv_out, gla_gate_w2,
                          gla_gate_b, gla_norm_g, w_gla_out, w_mix_out, xa_w_q, xa_w_kv, xa_w_o,
                          ffn2_w_gu, ffn2_w_down, ln_g, ln_b) for l in range(DEPTH)]
    return (_trunk(x_prompt, mem_prompt, layers), _trunk(x_sample, mem_sample, layers))
```

```python
import functools

import jax
import jax.numpy as jnp
from jax import lax
from jax.experimental import pallas as pl
from jax.experimental.pallas import tpu as pltpu

F32 = jnp.float32
BF16 = jnp.bfloat16

D_MODEL = 1024
DEPTH = 4
D_FF = 2816
D_CONV = D_MODEL // 2
GLA_HEADS = 4
D_GLA_V = D_MODEL // 2
HEAD_V = D_GLA_V // GLA_HEADS
D_GLA_K = D_GLA_V // 2
HEAD_K = D_GLA_K // GLA_HEADS
GATE_RANK = 16
GATE_TEMP = 16.0
CHUNK = 64
XA_HEADS = 4
XA_HEAD_DIM = D_MODEL // XA_HEADS
LN_EPS = 1e-5
RMS_EPS = 1e-6
DN_ALPHA = (2 * DEPTH) ** 0.25

SUBLANES = 8
LANES = 128
VMEM_LIMIT_BYTES = 56 * 1024 * 1024

TOKEN_TILE = 1024
TIME_TILE = 512
MXU_DIM = 256
XA_TILE = 1024
XA_ROW_SUB = 512
FFN_ROW_SUB = 512
FF_CHUNKS = (6 * MXU_DIM, 5 * MXU_DIM)
assert sum(FF_CHUNKS) == D_FF
LR_PAD = LANES

_H0, _GC0, _GB0 = 0, D_CONV, 2 * D_CONV
_Q0 = 3 * D_CONV
_K0 = _Q0 + D_GLA_K
_V0 = _K0 + D_GLA_K
_G0 = _V0 + D_GLA_V
D_PROJ = _G0 + D_GLA_V


def _dot(a, b):
    return jnp.dot(a, b, preferred_element_type=F32)


def _dot_nt(a, b):
    return lax.dot_general(a, b, (((1,), (1,)), ((), ())), preferred_element_type=F32)


def _residual_layer_norm(x, y, g, b):
    z = DN_ALPHA * x + y
    mu = jnp.mean(z, axis=-1, keepdims=True)
    zc = z - mu
    var = jnp.mean(zc * zc, axis=-1, keepdims=True)
    return zc * lax.rsqrt(var + LN_EPS) * g + b


def _const_spec(shape):
    n = len(shape)
    return pl.BlockSpec(shape, lambda *_: (0,) * n, pipeline_mode=pl.Buffered(1))


def _ffn_kernel(x_ref, wgu_ref, wd_ref, g_ref, b_ref, o_ref):
    for r0 in range(0, x_ref.shape[0], FFN_ROW_SUB):
        x = x_ref[r0:r0 + FFN_ROW_SUB, :]
        xb = x.astype(BF16)
        acc = None
        c0 = 0
        for width in FF_CHUNKS:
            gate = _dot(xb, wgu_ref[:, c0:c0 + width])
            up = _dot(xb, wgu_ref[:, D_FF + c0:D_FF + c0 + width])
            act = (gate * jax.nn.sigmoid(gate)) * up
            part = _dot(act.astype(BF16), wd_ref[c0:c0 + width, :])
            acc = part if acc is None else acc + part
            c0 += width
        o_ref[r0:r0 + FFN_ROW_SUB, :] = _residual_layer_norm(x, 0.5 * acc, g_ref[...], b_ref[...])


def _ffn_call(x2d, w_gu, w_down, ln_g, ln_b):
    n = x2d.shape[0]
    assert n % TOKEN_TILE == 0
    return pl.pallas_call(
        _ffn_kernel,
        grid=(n // TOKEN_TILE,),
        in_specs=[
            pl.BlockSpec((TOKEN_TILE, D_MODEL), lambda i: (i, 0)),
            _const_spec((D_MODEL, 2 * D_FF)),
            _const_spec((D_FF, D_MODEL)),
            _const_spec((1, D_MODEL)),
            _const_spec((1, D_MODEL)),
        ],
        out_specs=pl.BlockSpec((TOKEN_TILE, D_MODEL), lambda i: (i, 0)),
        out_shape=jax.ShapeDtypeStruct((n, D_MODEL), F32),
        compiler_params=pltpu.CompilerParams(
            dimension_semantics=("arbitrary",), vmem_limit_bytes=VMEM_LIMIT_BYTES),
        name="ffn_ln",
    )(x2d, w_gu, w_down, ln_g, ln_b)


def _log_decay(xb, wlr_ref, w2_ref, b2_ref):
    lr = _dot(xb, wlr_ref[...])
    z = _dot(lr.astype(BF16), w2_ref[...]) + b2_ref[...]
    return jax.nn.log_sigmoid(z) / GATE_TEMP


def _gla_tile(q_ref, k_ref, v_ref, la_ref, o_ref, st_ref, *, q0, k0, v0, reverse):
    rows = q_ref.shape[0]
    n_chunks = rows // CHUNK
    ri = lax.broadcasted_iota(jnp.int32, (CHUNK, CHUNK), 0)
    ci = lax.broadcasted_iota(jnp.int32, (CHUNK, CHUNK), 1)
    stacked = GLA_HEADS * CHUNK
    sr = lax.broadcasted_iota(jnp.int32, (stacked, stacked), 0)
    sc = lax.broadcasted_iota(jnp.int32, (stacked, stacked), 1)
    same_head = (sr // CHUNK) == (sc // CHUNK)
    if reverse:
        cum = (ci >= ri).astype(BF16)
        att_mask = same_head & ((sc % CHUNK) > (sr % CHUNK))
        ref_row, tot_row = CHUNK // 2, 0
    else:
        cum = (ci <= ri).astype(BF16)
        att_mask = same_head & ((sc % CHUNK) <= (sr % CHUNK))
        ref_row, tot_row = CHUNK // 2 - 1, CHUNK - 1
    head_of_lane = lax.broadcasted_iota(jnp.int32, (1, D_GLA_K), 1) // HEAD_K
    head_masks = [(head_of_lane == h).astype(BF16) for h in range(GLA_HEADS)]

    def stack_heads(a):
        ab = a.astype(BF16)
        return jnp.concatenate([ab * m for m in head_masks], axis=0)

    for step in range(n_chunks):
        r0 = (n_chunks - 1 - step if reverse else step) * CHUNK
        q = q_ref[r0:r0 + CHUNK, q0:q0 + D_GLA_K] * (HEAD_K ** -0.5)
        k = k_ref[r0:r0 + CHUNK, k0:k0 + D_GLA_K]
        v = v_ref[r0:r0 + CHUNK, v0:v0 + D_GLA_V]
        la = la_ref[r0:r0 + CHUNK, :]
        la_hi = la.astype(BF16)
        la_lo = (la - la_hi.astype(F32)).astype(BF16)
        b = _dot(cum, la_hi) + _dot(cum, la_lo)
        b_ref = b[ref_row:ref_row + 1, :]
        b_tot = b[tot_row:tot_row + 1, :]
        q_in = q * jnp.exp(b - b_ref)
        k_in = (k * jnp.exp(b_ref - b)).astype(BF16)
        k_dec = k * jnp.exp(b_tot - b)
        q_dec = q * jnp.exp(b)
        att = _dot_nt(stack_heads(q_in), jnp.concatenate([k_in] * GLA_HEADS, axis=0))
        att = jnp.where(att_mask, att, 0.0).astype(BF16)
        v_st = jnp.concatenate([v[:, h * HEAD_V:(h + 1) * HEAD_V] for h in range(GLA_HEADS)], axis=0)
        state = st_ref[...]
        o_st = _dot(att, v_st.astype(BF16)) + _dot_nt(stack_heads(q_dec), state.astype(BF16))
        kv_t = _dot(v_st.T.astype(BF16), stack_heads(k_dec))
        st_ref[...] = state * jnp.exp(b_tot) + kv_t
        o_ref[r0:r0 + CHUNK, :] = jnp.concatenate(
            [o_st[h * CHUNK:(h + 1) * CHUNK, :] for h in range(GLA_HEADS)], axis=1)


def _gla_bwd_kernel(x_ref, wqkv_ref, wlr_ref, w2_ref, b2_ref, o_ref, p_ref, la_ref, st_ref):
    @pl.when(pl.program_id(1) == 0)
    def _():
        st_ref[...] = jnp.zeros(st_ref.shape, F32)

    xb = x_ref[0].astype(BF16)
    p_ref[...] = _dot(xb, wqkv_ref[...])
    la_ref[...] = _log_decay(xb, wlr_ref, w2_ref, b2_ref)
    _gla_tile(p_ref, p_ref, p_ref, la_ref, o_ref.at[0], st_ref,
              q0=0, k0=D_GLA_K, v0=2 * D_GLA_K, reverse=True)


def _gla_bwd_call(x, w_qkv, w_lr, w2_b, b2_b):
    bsz, seq, _ = x.shape
    nt = seq // TIME_TILE
    rev = lambda b, i: (b, nt - 1 - i, 0)
    return pl.pallas_call(
        _gla_bwd_kernel,
        grid=(bsz, nt),
        in_specs=[
            pl.BlockSpec((1, TIME_TILE, D_MODEL), rev),
            _const_spec((D_MODEL, 2 * D_GLA_K + D_GLA_V)),
            _const_spec((D_MODEL, LR_PAD)),
            _const_spec((LR_PAD, D_GLA_K)),
            _const_spec((1, D_GLA_K)),
        ],
        out_specs=pl.BlockSpec((1, TIME_TILE, D_GLA_V), rev),
        out_shape=jax.ShapeDtypeStruct((bsz, seq, D_GLA_V), F32),
        scratch_shapes=[
            pltpu.VMEM((TIME_TILE, 2 * D_GLA_K + D_GLA_V), F32),
            pltpu.VMEM((TIME_TILE, D_GLA_K), F32),
            pltpu.VMEM((HEAD_V, D_GLA_K), F32),
        ],
        compiler_params=pltpu.CompilerParams(
            dimension_semantics=("arbitrary", "arbitrary"), vmem_limit_bytes=VMEM_LIMIT_BYTES),
        name="gla_bwd",
    )(x, w_qkv, w_lr, w2_b, b2_b)


def _mixer_kernel(x_ref, xprev_ref, xnext_ref, ob_ref, wp_ref, wlr_ref, w2_ref, b2_ref, wm_ref,
                  convw_ref, wco_ref, gn_ref, wgo_ref, wout_ref, g_ref, b_ref,
                  o_ref, p_ref, la_ref, of_ref, st_ref):
    i = pl.program_id(1)
    nt = pl.num_programs(1)

    @pl.when(i == 0)
    def _():
        st_ref[...] = jnp.zeros(st_ref.shape, F32)

    x = x_ref[0]
    xb = x.astype(BF16)
    p_ref[...] = _dot(xb, wp_ref[...])
    la_ref[...] = _log_decay(xb, wlr_ref, w2_ref, b2_ref)

    _gla_tile(p_ref, p_ref, p_ref, la_ref, of_ref, st_ref, q0=_Q0, k0=_K0, v0=_V0, reverse=False)
    gn = gn_ref[...]
    normed = []
    for h in range(GLA_HEADS):
        sl = slice(h * HEAD_V, (h + 1) * HEAD_V)
        o_h = of_ref[:, sl] + ob_ref[0, :, sl]
        ms = jnp.mean(o_h * o_h, axis=-1, keepdims=True)
        normed.append(o_h * lax.rsqrt(ms + RMS_EPS) * gn)
    gate = p_ref[:, _G0:_G0 + D_GLA_V]
    o_gla = jnp.concatenate(normed, axis=1) * (gate * jax.nn.sigmoid(gate))
    y_gla = _dot(o_gla.astype(BF16), wgo_ref[...])

    u = p_ref[:, _GC0:_GC0 + D_CONV] * p_ref[:, _H0:_H0 + D_CONV]
    xh = jnp.concatenate([xprev_ref[0], xnext_ref[0]], axis=0).astype(BF16)
    ph = _dot(xh, wp_ref[:, 0:2 * D_CONV])
    uh = ph[:, _GC0:_GC0 + D_CONV] * ph[:, _H0:_H0 + D_CONV]
    u_before = jnp.where(i > 0, uh[SUBLANES - 1:SUBLANES, :], 0.0)
    u_after = jnp.where(i < nt - 1, uh[SUBLANES:SUBLANES + 1, :], 0.0)
    rows = u.shape[0]
    row = lax.broadcasted_iota(jnp.int32, (rows, 1), 0)
    u_m1 = jnp.where(row == 0, u_before, pltpu.roll(u, 1, axis=0))
    u_p1 = jnp.where(row == rows - 1, u_after, pltpu.roll(u, rows - 1, axis=0))
    cw = convw_ref[...]
    conv = u_m1 * cw[0:1, :] + u * cw[1:2, :] + u_p1 * cw[2:3, :]
    y_conv = _dot((p_ref[:, _GB0:_GB0 + D_CONV] * conv).astype(BF16), wco_ref[...])

    mg = _dot(xb, wm_ref[...])
    merged = (jax.nn.sigmoid(mg[:, :D_MODEL]) * y_conv
              + jax.nn.sigmoid(mg[:, D_MODEL:]) * y_gla)
    y = _dot(merged.astype(BF16), wout_ref[...])
    o_ref[0] = _residual_layer_norm(x, y, g_ref[...], b_ref[...])


def _mixer_call(x, o_b, w):
    bsz, seq, _ = x.shape
    nt = seq // TIME_TILE
    per8 = TIME_TILE // SUBLANES
    last8 = seq // SUBLANES - 1
    tile = lambda b, i: (b, i, 0)
    return pl.pallas_call(
        _mixer_kernel,
        grid=(bsz, nt),
        in_specs=[
            pl.BlockSpec((1, TIME_TILE, D_MODEL), tile),
            pl.BlockSpec((1, SUBLANES, D_MODEL), lambda b, i: (b, jnp.maximum(i * per8 - 1, 0), 0)),
            pl.BlockSpec((1, SUBLANES, D_MODEL), lambda b, i: (b, jnp.minimum((i + 1) * per8, last8), 0)),
            pl.BlockSpec((1, TIME_TILE, D_GLA_V), tile),
            _const_spec((D_MODEL, D_PROJ)),
            _const_spec((D_MODEL, LR_PAD)),
            _const_spec((LR_PAD, D_GLA_K)),
            _const_spec((1, D_GLA_K)),
            _const_spec((D_MODEL, 2 * D_MODEL)),
            _const_spec((3, D_CONV)),
            _const_spec((D_CONV, D_MODEL)),
            _const_spec((1, HEAD_V)),
            _const_spec((D_GLA_V, D_MODEL)),
            _const_spec((D_MODEL, D_MODEL)),
            _const_spec((1, D_MODEL)),
            _const_spec((1, D_MODEL)),
        ],
        out_specs=pl.BlockSpec((1, TIME_TILE, D_MODEL), tile),
        out_shape=jax.ShapeDtypeStruct((bsz, seq, D_MODEL), F32),
        scratch_shapes=[
            pltpu.VMEM((TIME_TILE, D_PROJ), F32),
            pltpu.VMEM((TIME_TILE, D_GLA_K), F32),
            pltpu.VMEM((TIME_TILE, D_GLA_V), F32),
            pltpu.VMEM((HEAD_V, D_GLA_K), F32),
        ],
        compiler_params=pltpu.CompilerParams(
            dimension_semantics=("arbitrary", "arbitrary"), vmem_limit_bytes=VMEM_LIMIT_BYTES),
        name="mixer_ln",
    )(x, x, x, o_b, w["w_proj"], w["w_lr"], w["w2_f"], w["b2_f"], w["w_merge"], w["conv_w"],
      w["w_conv_out"], w["gla_norm_g"], w["w_gla_out"], w["w_mix_out"], w["ln_g1"], w["ln_b1"])


def _xattn_kernel(x_ref, mem_ref, wq_ref, wkv_ref, wo_ref, g_ref, b_ref, o_ref, kv_ref):
    @pl.when(pl.program_id(1) == 0)
    def _():
        kv_ref[...] = _dot(mem_ref[0].astype(BF16), wkv_ref[...]).astype(BF16)

    for r0 in range(0, x_ref.shape[1], XA_ROW_SUB):
        x = x_ref[0, r0:r0 + XA_ROW_SUB, :]
        q = _dot(x.astype(BF16), wq_ref[...]).astype(BF16)
        heads = []
        for h in range(XA_HEADS):
            sl = slice(h * XA_HEAD_DIM, (h + 1) * XA_HEAD_DIM)
            k_h = kv_ref[:, sl]
            v_h = kv_ref[:, D_MODEL + h * XA_HEAD_DIM:D_MODEL + (h + 1) * XA_HEAD_DIM]
            s = _dot_nt(q[:, sl], k_h) * (XA_HEAD_DIM ** -0.5)
            e = jnp.exp(s - jnp.max(s, axis=-1, keepdims=True))
            p = e / jnp.sum(e, axis=-1, keepdims=True)
            heads.append(_dot(p.astype(BF16), v_h))
        o = jnp.concatenate(heads, axis=1)
        y = _dot(o.astype(BF16), wo_ref[...])
        o_ref[0, r0:r0 + XA_ROW_SUB, :] = _residual_layer_norm(x, y, g_ref[...], b_ref[...])


def _xattn_call(x, mem, w_q, w_kv, w_o, ln_g, ln_b):
    bsz, seq, _ = x.shape
    n_mem = mem.shape[1]
    tile = lambda b, i: (b, i, 0)
    return pl.pallas_call(
        _xattn_kernel,
        grid=(bsz, seq // XA_TILE),
        in_specs=[
            pl.BlockSpec((1, XA_TILE, D_MODEL), tile),
            pl.BlockSpec((1, n_mem, D_MODEL), lambda b, i: (b, 0, 0)),
            _const_spec((D_MODEL, D_MODEL)),
            _const_spec((D_MODEL, 2 * D_MODEL)),
            _const_spec((D_MODEL, D_MODEL)),
            _const_spec((1, D_MODEL)),
            _const_spec((1, D_MODEL)),
        ],
        out_specs=pl.BlockSpec((1, XA_TILE, D_MODEL), tile),
        out_shape=jax.ShapeDtypeStruct((bsz, seq, D_MODEL), F32),
        scratch_shapes=[pltpu.VMEM((n_mem, 2 * D_MODEL), BF16)],
        compiler_params=pltpu.CompilerParams(
            dimension_semantics=("arbitrary", "arbitrary"), vmem_limit_bytes=VMEM_LIMIT_BYTES),
        name="xattn_ln",
    )(x, mem, w_q, w_kv, w_o, ln_g, ln_b)


def _pack_layer(l, ffn1_w_gu, ffn1_w_down, w_mix_in, conv_w, w_conv_out, gla_gate_w2, gla_gate_b,
                gla_norm_g, w_gla_out, w_mix_out, xa_w_q, xa_w_kv, xa_w_o, ffn2_w_gu, ffn2_w_down,
                ln_g, ln_b):
    w_in = w_mix_in[l]
    o = 0
    cols = {}
    for name, width in (("h", D_CONV), ("gb", D_CONV), ("gc", D_CONV), ("q", D_GLA_K), ("k", D_GLA_K),
                        ("v", D_GLA_V), ("g", D_GLA_V), ("lr_f", GATE_RANK), ("lr_b", GATE_RANK),
                        ("m_conv", D_MODEL), ("m_gla", D_MODEL)):
        cols[name] = w_in[:, o:o + width]
        o += width
    cat = lambda names: jnp.concatenate([cols[n] for n in names], axis=1).astype(BF16)
    w_lr = jnp.pad(jnp.concatenate([cols["lr_f"], cols["lr_b"]], axis=1),
                   ((0, 0), (0, LR_PAD - 2 * GATE_RANK))).astype(BF16)
    w2 = gla_gate_w2[l]
    w2_f = jnp.pad(w2[0], ((0, LR_PAD - GATE_RANK), (0, 0))).astype(BF16)
    w2_b = jnp.pad(w2[1], ((GATE_RANK, LR_PAD - 2 * GATE_RANK), (0, 0))).astype(BF16)
    row = lambda a: a.reshape(1, -1)
    return dict(
        ffn1_w_gu=ffn1_w_gu[l].astype(BF16), ffn1_w_down=ffn1_w_down[l].astype(BF16),
        ffn2_w_gu=ffn2_w_gu[l].astype(BF16), ffn2_w_down=ffn2_w_down[l].astype(BF16),
        w_proj=cat(("h", "gc", "gb", "q", "k", "v", "g")), w_qkv=cat(("q", "k", "v")),
        w_lr=w_lr, w2_f=w2_f, w2_b=w2_b,
        b2_f=row(gla_gate_b[l, 0]), b2_b=row(gla_gate_b[l, 1]),
        w_merge=cat(("m_conv", "m_gla")), conv_w=conv_w[l],
        w_conv_out=w_conv_out[l].astype(BF16), gla_norm_g=row(gla_norm_g[l]),
        w_gla_out=w_gla_out[l].astype(BF16), w_mix_out=w_mix_out[l].astype(BF16),
        xa_w_q=xa_w_q[l].astype(BF16), xa_w_kv=xa_w_kv[l].astype(BF16), xa_w_o=xa_w_o[l].astype(BF16),
        ln_g0=row(ln_g[l, 0]), ln_b0=row(ln_b[l, 0]), ln_g1=row(ln_g[l, 1]), ln_b1=row(ln_b[l, 1]),
        ln_g2=row(ln_g[l, 2]), ln_b2=row(ln_b[l, 2]), ln_g3=row(ln_g[l, 3]), ln_b3=row(ln_b[l, 3]),
    )


def _trunk(x, mem, layers):
    bsz, seq, d = x.shape
    for w in layers:
        x = _ffn_call(x.reshape(bsz * seq, d), w["ffn1_w_gu"], w["ffn1_w_down"],
                      w["ln_g0"], w["ln_b0"]).reshape(bsz, seq, d)
        o_b = _gla_bwd_call(x, w["w_qkv"], w["w_lr"], w["w2_b"], w["b2_b"])
        x = _mixer_call(x, o_b, w)
        x = _xattn_call(x, mem, w["xa_w_q"], w["xa_w_kv"], w["xa_w_o"], w["ln_g2"], w["ln_b2"])
        x = _ffn_call(x.reshape(bsz * seq, d), w["ffn2_w_gu"], w["ffn2_w_down"],
                      w["ln_g3"], w["ln_b3"]).reshape(bsz, seq, d)
    return x


def kernel(x_prompt, x_sample, mem_prompt, mem_sample, ffn1_w_gu, ffn1_w_down, w_mix_in, conv_w, w_conv_out, gla_gate_w2, gla_gate_b, gla_norm_g, w_gla_out, w_mix_out, xa_w_q, xa_w_kv, xa_w_o, ffn2_w_gu, ffn2_w_down, ln_g, ln_b):
    layers = [_pack_layer(l, ffn1_w_gu, ffn1_w_down, w_mix_in, conv_w, w_conv_out, gla_gate_w2,
                          gla_gate_b, gla_norm_g, w_gla_out, w_mix_out, xa_w_q, xa_w_kv, xa_w_o,
                          ffn2_w_gu, ffn2_w_down, ln_g, ln_b) for l in range(DEPTH)]
    return (_trunk(x_prompt, mem_prompt, layers), _trunk(x_sample, mem_sample, layers))
```

```python
import functools

import jax
import jax.numpy as jnp
from jax import lax
from jax.experimental import pallas as pl
from jax.experimental.pallas import tpu as pltpu

F32 = jnp.float32
BF16 = jnp.bfloat16

D_MODEL = 1024
DEPTH = 4
D_FF = 2816
D_CONV = D_MODEL // 2
GLA_HEADS = 4
D_GLA_V = D_MODEL // 2
HEAD_V = D_GLA_V // GLA_HEADS
D_GLA_K = D_GLA_V // 2
HEAD_K = D_GLA_K // GLA_HEADS
GATE_RANK = 16
GATE_TEMP = 16.0
CHUNK = 64
XA_HEADS = 4
XA_HEAD_DIM = D_MODEL // XA_HEADS
LN_EPS = 1e-5
RMS_EPS = 1e-6
DN_ALPHA = (2 * DEPTH) ** 0.25

SUBLANES = 8
LANES = 128
VMEM_LIMIT_BYTES = 56 * 1024 * 1024

TOKEN_TILE = 1024
TIME_TILE = 512
MXU_DIM = 256
XA_TILE = 1024
XA_ROW_SUB = 512
FFN_ROW_SUB = 512
FF_CHUNKS = (6 * MXU_DIM, 5 * MXU_DIM)
assert sum(FF_CHUNKS) == D_FF
LR_PAD = LANES

_H0, _GC0, _GB0 = 0, D_CONV, 2 * D_CONV
_Q0 = 3 * D_CONV
_K0 = _Q0 + D_GLA_K
_V0 = _K0 + D_GLA_K
_G0 = _V0 + D_GLA_V
D_PROJ = _G0 + D_GLA_V


def _dot(a, b):
    return jnp.dot(a, b, preferred_element_type=F32)


def _dot_nt(a, b):
    return lax.dot_general(a, b, (((1,), (1,)), ((), ())), preferred_element_type=F32)


def _residual_layer_norm(x, y, g, b):
    z = DN_ALPHA * x + y
    mu = jnp.mean(z, axis=-1, keepdims=True)
    zc = z - mu
    var = jnp.mean(zc * zc, axis=-1, keepdims=True)
    return zc * lax.rsqrt(var + LN_EPS) * g + b


def _const_spec(shape):
    n = len(shape)
    return pl.BlockSpec(shape, lambda *_: (0,) * n, pipeline_mode=pl.Buffered(1))


def _ffn_kernel(x_ref, wgu_ref, wd_ref, g_ref, b_ref, o_ref):
    for r0 in range(0, x_ref.shape[0], FFN_ROW_SUB):
        x = x_ref[r0:r0 + FFN_ROW_SUB, :]
        xb = x.astype(BF16)
        acc = None
        c0 = 0
        for width in FF_CHUNKS:
            gate = _dot(xb, wgu_ref[:, c0:c0 + width])
            up = _dot(xb, wgu_ref[:, D_FF + c0:D_FF + c0 + width])
            act = (gate * jax.nn.sigmoid(gate)) * up
            part = _dot(act.astype(BF16), wd_ref[c0:c0 + width, :])
            acc = part if acc is None else acc + part
            c0 += width
        o_ref[r0:r0 + FFN_ROW_SUB, :] = _residual_layer_norm(x, 0.5 * acc, g_ref[...], b_ref[...])


def _ffn_call(x2d, w_gu, w_down, ln_g, ln_b):
    n = x2d.shape[0]
    assert n % TOKEN_TILE == 0
    return pl.pallas_call(
        _ffn_kernel,
        grid=(n // TOKEN_TILE,),
        in_specs=[
            pl.BlockSpec((TOKEN_TILE, D_MODEL), lambda i: (i, 0)),
            _const_spec((D_MODEL, 2 * D_FF)),
            _const_spec((D_FF, D_MODEL)),
            _const_spec((1, D_MODEL)),
            _const_spec((1, D_MODEL)),
        ],
        out_specs=pl.BlockSpec((TOKEN_TILE, D_MODEL), lambda i: (i, 0)),
        out_shape=jax.ShapeDtypeStruct((n, D_MODEL), F32),
        compiler_params=pltpu.CompilerParams(
            dimension_semantics=("arbitrary",), vmem_limit_bytes=VMEM_LIMIT_BYTES),
        name="ffn_ln",
    )(x2d, w_gu, w_down, ln_g, ln_b)


def _log_decay(xb, wlr_ref, w2_ref, b2_ref):
    lr = _dot(xb, wlr_ref[...])
    z = _dot(lr.astype(BF16), w2_ref[...]) + b2_ref[...]
    return jax.nn.log_sigmoid(z) / GATE_TEMP


def _gla_scratch(rows):
    n = rows // CHUNK
    stacked = GLA_HEADS * CHUNK
    return [
        pltpu.VMEM((n * stacked, D_GLA_K), BF16),
        pltpu.VMEM((n * stacked, D_GLA_K), BF16),
        pltpu.VMEM((n * stacked, D_GLA_K), BF16),
        pltpu.VMEM((rows, D_GLA_K), BF16),
        pltpu.VMEM((n * stacked, HEAD_V), BF16),
        pltpu.VMEM((n * HEAD_V, stacked), BF16),
        pltpu.VMEM((n * stacked, stacked), BF16),
        pltpu.VMEM((n * HEAD_V, D_GLA_K), F32),
        pltpu.VMEM((n * HEAD_V, D_GLA_K), BF16),
    ]


def _gla_tile(q_ref, k_ref, v_ref, la_ref, o_ref, st_ref, scratch, *, q0, k0, v0, reverse,
              interleave=None):
    qin_ref, qdec_ref, kdec_ref, kin_ref, vst_ref, vstt_ref, att_ref, kv_ref, sb_ref = scratch
    rows = q_ref.shape[0]
    n_chunks = rows // CHUNK
    ri = lax.broadcasted_iota(jnp.int32, (CHUNK, CHUNK), 0)
    ci = lax.broadcasted_iota(jnp.int32, (CHUNK, CHUNK), 1)
    stacked = GLA_HEADS * CHUNK
    sr = lax.broadcasted_iota(jnp.int32, (stacked, stacked), 0)
    sc = lax.broadcasted_iota(jnp.int32, (stacked, stacked), 1)
    same_head = (sr // CHUNK) == (sc // CHUNK)
    if reverse:
        cum = (ci >= ri).astype(BF16)
        att_mask = same_head & ((sc % CHUNK) > (sr % CHUNK))
        ref_row, tot_row = CHUNK // 2, 0
    else:
        cum = (ci <= ri).astype(BF16)
        att_mask = same_head & ((sc % CHUNK) <= (sr % CHUNK))
        ref_row, tot_row = CHUNK // 2 - 1, CHUNK - 1
    head_of_lane = lax.broadcasted_iota(jnp.int32, (1, D_GLA_K), 1) // HEAD_K
    head_masks = [(head_of_lane == h).astype(BF16) for h in range(GLA_HEADS)]

    def stack_heads(a):
        ab = a.astype(BF16)
        return jnp.concatenate([ab * m for m in head_masks], axis=0)

    tok = lambda c: slice(c * CHUNK, (c + 1) * CHUNK)
    stk = lambda c: slice(c * stacked, (c + 1) * stacked)
    vrows = lambda c: slice(c * HEAD_V, (c + 1) * HEAD_V)

    cums = []
    for c in range(n_chunks):
        la = la_ref[tok(c), :]
        la_hi = la.astype(BF16)
        la_lo = (la - la_hi.astype(F32)).astype(BF16)
        cums.append(_dot(cum, la_hi) + _dot(cum, la_lo))

    decays = []
    for c in range(n_chunks):
        b = cums[c]
        b_ref = b[ref_row:ref_row + 1, :]
        b_tot = b[tot_row:tot_row + 1, :]
        q = q_ref[tok(c), q0:q0 + D_GLA_K] * (HEAD_K ** -0.5)
        k = k_ref[tok(c), k0:k0 + D_GLA_K]
        v = v_ref[tok(c), v0:v0 + D_GLA_V]
        qin_ref[stk(c), :] = stack_heads(q * jnp.exp(b - b_ref))
        kin_ref[tok(c), :] = (k * jnp.exp(b_ref - b)).astype(BF16)
        kdec_ref[stk(c), :] = stack_heads(k * jnp.exp(b_tot - b))
        qdec_ref[stk(c), :] = stack_heads(q * jnp.exp(b))
        v_st = jnp.concatenate([v[:, h * HEAD_V:(h + 1) * HEAD_V] for h in range(GLA_HEADS)], axis=0)
        vst_ref[stk(c), :] = v_st.astype(BF16)
        vstt_ref[vrows(c), :] = v_st.T.astype(BF16)
        decays.append(jnp.exp(b_tot))

    for c in range(n_chunks):
        k_in = kin_ref[tok(c), :]
        att = _dot_nt(qin_ref[stk(c), :], jnp.concatenate([k_in] * GLA_HEADS, axis=0))
        att_ref[stk(c), :] = jnp.where(att_mask, att, 0.0).astype(BF16)
        kv_ref[vrows(c), :] = _dot(vstt_ref[vrows(c), :], kdec_ref[stk(c), :])
        if interleave is not None:
            interleave(c)

    state = st_ref[...]
    for step in range(n_chunks):
        c = n_chunks - 1 - step if reverse else step
        sb_ref[vrows(c), :] = state.astype(BF16)
        state = state * decays[c] + kv_ref[vrows(c), :]
    st_ref[...] = state

    for c in range(n_chunks):
        o_st = _dot(att_ref[stk(c), :], vst_ref[stk(c), :]) + _dot_nt(qdec_ref[stk(c), :], sb_ref[vrows(c), :])
        o_ref[tok(c), :] = jnp.concatenate(
            [o_st[h * CHUNK:(h + 1) * CHUNK, :] for h in range(GLA_HEADS)], axis=1)


def _gla_bwd_kernel(x_ref, xnext_ref, wqkv_ref, wlr_ref, w2_ref, b2_ref, o_ref,
                    p0_ref, la0_ref, p1_ref, la1_ref, st_ref, *gla_scratch):
    step_no = pl.program_id(0) * pl.num_programs(1) + pl.program_id(1)

    @pl.when(step_no == 0)
    def _():
        xb = x_ref[0].astype(BF16)
        la0_ref[...] = _log_decay(xb, wlr_ref, w2_ref, b2_ref)
        p0_ref[...] = _dot(xb, wqkv_ref[...])

    @pl.when(pl.program_id(1) == 0)
    def _():
        st_ref[...] = jnp.zeros(st_ref.shape, F32)

    n_chunks = x_ref.shape[1] // CHUNK
    n_pieces = wqkv_ref.shape[1] // MXU_DIM
    stride = n_chunks // n_pieces
    assert stride >= 1 and 1 + (n_pieces - 1) * stride <= n_chunks - 1

    def run(p_ref, la_ref, p_next_ref, la_next_ref):
        xnb = xnext_ref[0].astype(BF16)

        def project_next_piece(step):
            if step == 0:
                la_next_ref[...] = _log_decay(xnb, wlr_ref, w2_ref, b2_ref)
            elif (step - 1) % stride == 0 and (step - 1) // stride < n_pieces:
                c0 = (step - 1) // stride * MXU_DIM
                p_next_ref[:, c0:c0 + MXU_DIM] = _dot(xnb, wqkv_ref[:, c0:c0 + MXU_DIM])

        _gla_tile(p_ref, p_ref, p_ref, la_ref, o_ref.at[0], st_ref, gla_scratch,
                  q0=0, k0=D_GLA_K, v0=2 * D_GLA_K, reverse=True, interleave=project_next_piece)

    parity = step_no & 1

    @pl.when(parity == 0)
    def _():
        run(p0_ref, la0_ref, p1_ref, la1_ref)

    @pl.when(parity == 1)
    def _():
        run(p1_ref, la1_ref, p0_ref, la0_ref)


def _gla_bwd_call(x, w_qkv, w_lr, w2_b, b2_b):
    bsz, seq, _ = x.shape
    nt = seq // TIME_TILE
    rev = lambda b, i: (b, nt - 1 - i, 0)

    def rev_next(b, i):
        nxt = jnp.minimum(b * nt + i + 1, bsz * nt - 1)
        return (nxt // nt, nt - 1 - nxt % nt, 0)

    return pl.pallas_call(
        _gla_bwd_kernel,
        grid=(bsz, nt),
        in_specs=[
            pl.BlockSpec((1, TIME_TILE, D_MODEL), rev),
            pl.BlockSpec((1, TIME_TILE, D_MODEL), rev_next),
            _const_spec((D_MODEL, 2 * D_GLA_K + D_GLA_V)),
            _const_spec((D_MODEL, LR_PAD)),
            _const_spec((LR_PAD, D_GLA_K)),
            _const_spec((1, D_GLA_K)),
        ],
        out_specs=pl.BlockSpec((1, TIME_TILE, D_GLA_V), rev),
        out_shape=jax.ShapeDtypeStruct((bsz, seq, D_GLA_V), F32),
        scratch_shapes=[
            pltpu.VMEM((TIME_TILE, 2 * D_GLA_K + D_GLA_V), F32),
            pltpu.VMEM((TIME_TILE, D_GLA_K), F32),
            pltpu.VMEM((TIME_TILE, 2 * D_GLA_K + D_GLA_V), F32),
            pltpu.VMEM((TIME_TILE, D_GLA_K), F32),
            pltpu.VMEM((HEAD_V, D_GLA_K), F32),
        ] + _gla_scratch(TIME_TILE),
        compiler_params=pltpu.CompilerParams(
            dimension_semantics=("arbitrary", "arbitrary"), vmem_limit_bytes=VMEM_LIMIT_BYTES),
        name="gla_bwd",
    )(x, x, w_qkv, w_lr, w2_b, b2_b)


def _mixer_kernel(x_ref, xprev_ref, xnext_ref, ob_ref, wp_ref, wlr_ref, w2_ref, b2_ref, wm_ref,
                  convw_ref, wco_ref, gn_ref, wgo_ref, wout_ref, g_ref, b_ref,
                  o_ref, p_ref, la_ref, of_ref, mg_ref, st_ref, *gla_scratch):
    i = pl.program_id(1)
    nt = pl.num_programs(1)

    @pl.when(i == 0)
    def _():
        st_ref[...] = jnp.zeros(st_ref.shape, F32)

    x = x_ref[0]
    xb = x.astype(BF16)
    la_ref[...] = _log_decay(xb, wlr_ref, w2_ref, b2_ref)
    p_ref[...] = _dot(xb, wp_ref[...])

    n_chunks = x.shape[0] // CHUNK
    piece = 2 * D_MODEL // n_chunks
    assert piece % MXU_DIM == 0

    def merge_gate_piece(step):
        cols = slice(step * piece, (step + 1) * piece)
        mg_ref[:, cols] = _dot(xb, wm_ref[:, cols])

    _gla_tile(p_ref, p_ref, p_ref, la_ref, of_ref, st_ref, gla_scratch, q0=_Q0, k0=_K0, v0=_V0,
              reverse=False, interleave=merge_gate_piece)
    gn = gn_ref[...]
    normed = []
    for h in range(GLA_HEADS):
        sl = slice(h * HEAD_V, (h + 1) * HEAD_V)
        o_h = of_ref[:, sl] + ob_ref[0, :, sl]
        ms = jnp.mean(o_h * o_h, axis=-1, keepdims=True)
        normed.append(o_h * lax.rsqrt(ms + RMS_EPS) * gn)
    gate = p_ref[:, _G0:_G0 + D_GLA_V]
    o_gla = jnp.concatenate(normed, axis=1) * (gate * jax.nn.sigmoid(gate))
    y_gla = _dot(o_gla.astype(BF16), wgo_ref[...])

    u = p_ref[:, _GC0:_GC0 + D_CONV] * p_ref[:, _H0:_H0 + D_CONV]
    xh = jnp.concatenate([xprev_ref[0], xnext_ref[0]], axis=0).astype(BF16)
    ph = _dot(xh, wp_ref[:, 0:2 * D_CONV])
    uh = ph[:, _GC0:_GC0 + D_CONV] * ph[:, _H0:_H0 + D_CONV]
    u_before = jnp.where(i > 0, uh[SUBLANES - 1:SUBLANES, :], 0.0)
    u_after = jnp.where(i < nt - 1, uh[SUBLANES:SUBLANES + 1, :], 0.0)
    rows = u.shape[0]
    row = lax.broadcasted_iota(jnp.int32, (rows, 1), 0)
    u_m1 = jnp.where(row == 0, u_before, pltpu.roll(u, 1, axis=0))
    u_p1 = jnp.where(row == rows - 1, u_after, pltpu.roll(u, rows - 1, axis=0))
    cw = convw_ref[...]
    conv = u_m1 * cw[0:1, :] + u * cw[1:2, :] + u_p1 * cw[2:3, :]
    y_conv = _dot((p_ref[:, _GB0:_GB0 + D_CONV] * conv).astype(BF16), wco_ref[...])

    merged = (jax.nn.sigmoid(mg_ref[:, :D_MODEL]) * y_conv
              + jax.nn.sigmoid(mg_ref[:, D_MODEL:]) * y_gla)
    merged = merged.astype(BF16)
    half = rows // 2
    for r0 in (0, half):
        y = _dot(merged[r0:r0 + half, :], wout_ref[...])
        o_ref[0, r0:r0 + half, :] = _residual_layer_norm(x[r0:r0 + half, :], y, g_ref[...], b_ref[...])


def _mixer_call(x, o_b, w):
    bsz, seq, _ = x.shape
    nt = seq // TIME_TILE
    per8 = TIME_TILE // SUBLANES
    last8 = seq // SUBLANES - 1
    tile = lambda b, i: (b, i, 0)
    return pl.pallas_call(
        _mixer_kernel,
        grid=(bsz, nt),
        in_specs=[
            pl.BlockSpec((1, TIME_TILE, D_MODEL), tile),
            pl.BlockSpec((1, SUBLANES, D_MODEL), lambda b, i: (b, jnp.maximum(i * per8 - 1, 0), 0)),
            pl.BlockSpec((1, SUBLANES, D_MODEL), lambda b, i: (b, jnp.minimum((i + 1) * per8, last8), 0)),
            pl.BlockSpec((1, TIME_TILE, D_GLA_V), tile),
            _const_spec((D_MODEL, D_PROJ)),
            _const_spec((D_MODEL, LR_PAD)),
            _const_spec((LR_PAD, D_GLA_K)),
            _const_spec((1, D_GLA_K)),
            _const_spec((D_MODEL, 2 * D_MODEL)),
            _const_spec((3, D_CONV)),
            _const_spec((D_CONV, D_MODEL)),
            _const_spec((1, HEAD_V)),
            _const_spec((D_GLA_V, D_MODEL)),
            _const_spec((D_MODEL, D_MODEL)),
            _const_spec((1, D_MODEL)),
            _const_spec((1, D_MODEL)),
        ],
        out_specs=pl.BlockSpec((1, TIME_TILE, D_MODEL), tile),
        out_shape=jax.ShapeDtypeStruct((bsz, seq, D_MODEL), F32),
        scratch_shapes=[
            pltpu.VMEM((TIME_TILE, D_PROJ), F32),
            pltpu.VMEM((TIME_TILE, D_GLA_K), F32),
            pltpu.VMEM((TIME_TILE, D_GLA_V), F32),
            pltpu.VMEM((TIME_TILE, 2 * D_MODEL), F32),
            pltpu.VMEM((HEAD_V, D_GLA_K), F32),
        ] + _gla_scratch(TIME_TILE),
        compiler_params=pltpu.CompilerParams(
            dimension_semantics=("arbitrary", "arbitrary"), vmem_limit_bytes=VMEM_LIMIT_BYTES),
        name="mixer_ln",
    )(x, x, x, o_b, w["w_proj"], w["w_lr"], w["w2_f"], w["b2_f"], w["w_merge"], w["conv_w"],
      w["w_conv_out"], w["gla_norm_g"], w["w_gla_out"], w["w_mix_out"], w["ln_g1"], w["ln_b1"])


def _xattn_kernel(x_ref, mem_ref, wq_ref, wkv_ref, wo_ref, g_ref, b_ref, o_ref, kv_ref):
    @pl.when(pl.program_id(1) == 0)
    def _():
        kv_ref[...] = _dot(mem_ref[0].astype(BF16), wkv_ref[...]).astype(BF16)

    for r0 in range(0, x_ref.shape[1], XA_ROW_SUB):
        x = x_ref[0, r0:r0 + XA_ROW_SUB, :]
        q = _dot(x.astype(BF16), wq_ref[...]).astype(BF16)
        heads = []
        for h in range(XA_HEADS):
            sl = slice(h * XA_HEAD_DIM, (h + 1) * XA_HEAD_DIM)
            k_h = kv_ref[:, sl]
            v_h = kv_ref[:, D_MODEL + h * XA_HEAD_DIM:D_MODEL + (h + 1) * XA_HEAD_DIM]
            s = _dot_nt(q[:, sl], k_h) * (XA_HEAD_DIM ** -0.5)
            e = jnp.exp(s - jnp.max(s, axis=-1, keepdims=True))
            p = e / jnp.sum(e, axis=-1, keepdims=True)
            heads.append(_dot(p.astype(BF16), v_h))
        o = jnp.concatenate(heads, axis=1)
        y = _dot(o.astype(BF16), wo_ref[...])
        o_ref[0, r0:r0 + XA_ROW_SUB, :] = _residual_layer_norm(x, y, g_ref[...], b_ref[...])


def _xattn_call(x, mem, w_q, w_kv, w_o, ln_g, ln_b):
    bsz, seq, _ = x.shape
    n_mem = mem.shape[1]
    tile = lambda b, i: (b, i, 0)
    return pl.pallas_call(
        _xattn_kernel,
        grid=(bsz, seq // XA_TILE),
        in_specs=[
            pl.BlockSpec((1, XA_TILE, D_MODEL), tile),
            pl.BlockSpec((1, n_mem, D_MODEL), lambda b, i: (b, 0, 0)),
            _const_spec((D_MODEL, D_MODEL)),
            _const_spec((D_MODEL, 2 * D_MODEL)),
            _const_spec((D_MODEL, D_MODEL)),
            _const_spec((1, D_MODEL)),
            _const_spec((1, D_MODEL)),
        ],
        out_specs=pl.BlockSpec((1, XA_TILE, D_MODEL), tile),
        out_shape=jax.ShapeDtypeStruct((bsz, seq, D_MODEL), F32),
        scratch_shapes=[pltpu.VMEM((n_mem, 2 * D_MODEL), BF16)],
        compiler_params=pltpu.CompilerParams(
            dimension_semantics=("arbitrary", "arbitrary"), vmem_limit_bytes=VMEM_LIMIT_BYTES),
        name="xattn_ln",
    )(x, mem, w_q, w_kv, w_o, ln_g, ln_b)


def _pack_layer(l, ffn1_w_gu, ffn1_w_down, w_mix_in, conv_w, w_conv_out, gla_gate_w2, gla_gate_b,
                gla_norm_g, w_gla_out, w_mix_out, xa_w_q, xa_w_kv, xa_w_o, ffn2_w_gu, ffn2_w_down,
                ln_g, ln_b):
    w_in = w_mix_in[l]
    o = 0
    cols = {}
    for name, width in (("h", D_CONV), ("gb", D_CONV), ("gc", D_CONV), ("q", D_GLA_K), ("k", D_GLA_K),
                        ("v", D_GLA_V), ("g", D_GLA_V), ("lr_f", GATE_RANK), ("lr_b", GATE_RANK),
                        ("m_conv", D_MODEL), ("m_gla", D_MODEL)):
        cols[name] = w_in[:, o:o + width]
        o += width
    cat = lambda names: jnp.concatenate([cols[n] for n in names], axis=1).astype(BF16)
    w_lr = jnp.pad(jnp.concatenate([cols["lr_f"], cols["lr_b"]], axis=1),
                   ((0, 0), (0, LR_PAD - 2 * GATE_RANK))).astype(BF16)
    w2 = gla_gate_w2[l]
    w2_f = jnp.pad(w2[0], ((0, LR_PAD - GATE_RANK), (0, 0))).astype(BF16)
    w2_b = jnp.pad(w2[1], ((GATE_RANK, LR_PAD - 2 * GATE_RANK), (0, 0))).astype(BF16)
    row = lambda a: a.reshape(1, -1)
    return dict(
        ffn1_w_gu=ffn1_w_gu[l].astype(BF16), ffn1_w_down=ffn1_w_down[l].astype(BF16),
        ffn2_w_gu=ffn2_w_gu[l].astype(BF16), ffn2_w_down=ffn2_w_down[l].astype(BF16),
        w_proj=cat(("h", "gc", "gb", "q", "k", "v", "g")), w_qkv=cat(("q", "k", "v")),
        w_lr=w_lr, w2_f=w2_f, w2_b=w2_b,
        b2_f=row(gla_gate_b[l, 0]), b2_b=row(gla_gate_b[l, 1]),
        w_merge=cat(("m_conv", "m_gla")), conv_w=conv_w[l],
        w_conv_out=w_conv_out[l].astype(BF16), gla_norm_g=row(gla_norm_g[l]),
        w_gla_out=w_gla_out[l].astype(BF16), w_mix_out=w_mix_out[l].astype(BF16),
        xa_w_q=xa_w_q[l].astype(BF16), xa_w_kv=xa_w_kv[l].astype(BF16), xa_w_o=xa_w_o[l].astype(BF16),
        ln_g0=row(ln_g[l, 0]), ln_b0=row(ln_b[l, 0]), ln_g1=row(ln_g[l, 1]), ln_b1=row(ln_b[l, 1]),
        ln_g2=row(ln_g[l, 2]), ln_b2=row(ln_b[l, 2]), ln_g3=row(ln_g[l, 3]), ln_b3=row(ln_b[l, 3]),
    )


def _trunk(x, mem, layers):
    bsz, seq, d = x.shape
    for w in layers:
        x = _ffn_call(x.reshape(bsz * seq, d), w["ffn1_w_gu"], w["ffn1_w_down"],
                      w["ln_g0"], w["ln_b0"]).reshape(bsz, seq, d)
        o_b = _gla_bwd_call(x, w["w_qkv"], w["w_lr"], w["w2_b"], w["b2_b"])
        x = _mixer_call(x, o_b, w)
        x = _xattn_call(x, mem, w["xa_w_q"], w["xa_w_kv"], w["xa_w_o"], w["ln_g2"], w["ln_b2"])
        x = _ffn_call(x.reshape(bsz * seq, d), w["ffn2_w_gu"], w["ffn2_w_down"],
                      w["ln_g3"], w["ln_b3"]).reshape(bsz, seq, d)
    return x


def kernel(x_prompt, x_sample, mem_prompt, mem_sample, ffn1_w_gu, ffn1_w_down, w_mix_in, conv_w, w_conv_out, gla_gate_w2, gla_gate_b, gla_norm_g, w_gla_out, w_mix_out, xa_w_q, xa_w_kv, xa_w_o, ffn2_w_gu, ffn2_w_down, ln_g, ln_b):
    layers = [_pack_layer(l, ffn1_w_gu, ffn1_w_down, w_mix_in, conv_w, w_conv_out, gla_gate_w2,
                          gla_gate_b, gla_norm_g, w_gla_out, w_mix_out, xa_w_q, xa_w_kv, xa_w_o,
                          ffn2_w_gu, ffn2_w_down, ln_g, ln_b) for l in range(DEPTH)]
    return (_trunk(x_prompt, mem_prompt, layers), _trunk(x_sample, mem_sample, layers))
```

```python
import functools

import jax
import jax.numpy as jnp
from jax import lax
from jax.experimental import pallas as pl
from jax.experimental.pallas import tpu as pltpu

F32 = jnp.float32
BF16 = jnp.bfloat16

D_MODEL = 1024
DEPTH = 4
D_FF = 2816
D_CONV = D_MODEL // 2
GLA_HEADS = 4
D_GLA_V = D_MODEL // 2
HEAD_V = D_GLA_V // GLA_HEADS
D_GLA_K = D_GLA_V // 2
HEAD_K = D_GLA_K // GLA_HEADS
GATE_RANK = 16
GATE_TEMP = 16.0
CHUNK = 64
XA_HEADS = 4
XA_HEAD_DIM = D_MODEL // XA_HEADS
LN_EPS = 1e-5
RMS_EPS = 1e-6
DN_ALPHA = (2 * DEPTH) ** 0.25

SUBLANES = 8
LANES = 128
VMEM_LIMIT_BYTES = 56 * 1024 * 1024

TOKEN_TILE = 1024
TIME_TILE = 512
MXU_DIM = 256
XA_TILE = 1024
XA_ROW_SUB = 512
FFN_ROW_SUBS = (512, 512)
assert sum(FFN_ROW_SUBS) == TOKEN_TILE
FF_CHUNKS = (6 * MXU_DIM, 5 * MXU_DIM)
assert sum(FF_CHUNKS) == D_FF
LR_PAD = LANES

_H0, _GB0, _GC0 = 0, D_CONV, 2 * D_CONV
_Q0 = 3 * D_CONV
_K0 = _Q0 + D_GLA_K
_V0 = _K0 + D_GLA_K
_G0 = _V0 + D_GLA_V
D_PROJ = _G0 + D_GLA_V
_LR0 = D_PROJ
_MERGE0 = _LR0 + 2 * GATE_RANK
assert _Q0 % (2 * D_GLA_K) == 0 and _V0 % D_GLA_V == 0


def _dot(a, b):
    return jnp.dot(a, b, preferred_element_type=F32)


def _dot_nt(a, b):
    return lax.dot_general(a, b, (((1,), (1,)), ((), ())), preferred_element_type=F32)


def _residual_layer_norm(x, y, g, b):
    z = DN_ALPHA * x + y
    mu = jnp.mean(z, axis=-1, keepdims=True)
    zc = z - mu
    var = jnp.mean(zc * zc, axis=-1, keepdims=True)
    return zc * lax.rsqrt(var + LN_EPS) * g + b


def _weight_spec(arr, lead, block=None, index=None):
    lead = tuple(lead)
    tail = tuple(arr.shape[len(lead):]) if block is None else tuple(block)
    pos = (0,) * len(tail) if index is None else tuple(index)
    return pl.BlockSpec((None,) * len(lead) + tail, lambda *_: lead + pos, pipeline_mode=pl.Buffered(1))


def _ffn_kernel(x_ref, wgu_ref, wd_ref, g_ref, b_ref, o_ref):
    r0 = 0
    for sub in FFN_ROW_SUBS:
        x = x_ref[r0:r0 + sub, :]
        xb = x.astype(BF16)
        acc = None
        c0 = 0
        for width in FF_CHUNKS:
            gate = _dot(xb, wgu_ref[:, c0:c0 + width])
            up = _dot(xb, wgu_ref[:, D_FF + c0:D_FF + c0 + width])
            act = (gate * jax.nn.sigmoid(gate)) * up
            part = _dot(act.astype(BF16), wd_ref[c0:c0 + width, :])
            acc = part if acc is None else acc + part
            c0 += width
        o_ref[r0:r0 + sub, :] = _residual_layer_norm(x, 0.5 * acc, g_ref[...], b_ref[...])
        r0 += sub


def _ffn_call(x2d, w_gu, w_down, w, l, ln_idx):
    n = x2d.shape[0]
    assert n % TOKEN_TILE == 0
    return pl.pallas_call(
        _ffn_kernel,
        grid=(n // TOKEN_TILE,),
        in_specs=[
            pl.BlockSpec((TOKEN_TILE, D_MODEL), lambda i: (i, 0)),
            _weight_spec(w_gu, (l,)),
            _weight_spec(w_down, (l,)),
            _weight_spec(w["ln_g"], (l, ln_idx)),
            _weight_spec(w["ln_b"], (l, ln_idx)),
        ],
        out_specs=pl.BlockSpec((TOKEN_TILE, D_MODEL), lambda i: (i, 0)),
        out_shape=jax.ShapeDtypeStruct((n, D_MODEL), F32),
        compiler_params=pltpu.CompilerParams(
            dimension_semantics=("arbitrary",), vmem_limit_bytes=VMEM_LIMIT_BYTES),
        name="ffn_ln",
    )(x2d, w_gu, w_down, w["ln_g"], w["ln_b"])


def _log_decay(xb, wlr_ref, w2_ref, b2_ref):
    lr = _dot(xb, wlr_ref[...])
    z = _dot(lr.astype(BF16), w2_ref[...]) + b2_ref[...]
    return jax.nn.log_sigmoid(z) / GATE_TEMP


def _gla_scratch(rows):
    n = rows // CHUNK
    stacked = GLA_HEADS * CHUNK
    return [
        pltpu.VMEM((n * stacked, D_GLA_K), BF16),
        pltpu.VMEM((n * stacked, D_GLA_K), BF16),
        pltpu.VMEM((n * stacked, D_GLA_K), BF16),
        pltpu.VMEM((rows, D_GLA_K), BF16),
        pltpu.VMEM((n * stacked, HEAD_V), BF16),
        pltpu.VMEM((n * HEAD_V, stacked), BF16),
        pltpu.VMEM((n * stacked, stacked), BF16),
        pltpu.VMEM((n * HEAD_V, D_GLA_K), F32),
        pltpu.VMEM((n * HEAD_V, D_GLA_K), BF16),
    ]


def _gla_tile(q_ref, k_ref, v_ref, la_ref, o_ref, st_ref, scratch, *, q0, k0, v0, reverse,
              interleave=None):
    qin_ref, qdec_ref, kdec_ref, kin_ref, vst_ref, vstt_ref, att_ref, kv_ref, sb_ref = scratch
    rows = q_ref.shape[0]
    n_chunks = rows // CHUNK
    ri = lax.broadcasted_iota(jnp.int32, (CHUNK, CHUNK), 0)
    ci = lax.broadcasted_iota(jnp.int32, (CHUNK, CHUNK), 1)
    stacked = GLA_HEADS * CHUNK
    sr = lax.broadcasted_iota(jnp.int32, (stacked, stacked), 0)
    sc = lax.broadcasted_iota(jnp.int32, (stacked, stacked), 1)
    same_head = (sr // CHUNK) == (sc // CHUNK)
    if reverse:
        cum = (ci >= ri).astype(BF16)
        att_mask = same_head & ((sc % CHUNK) > (sr % CHUNK))
        ref_row, tot_row = CHUNK // 2, 0
    else:
        cum = (ci <= ri).astype(BF16)
        att_mask = same_head & ((sc % CHUNK) <= (sr % CHUNK))
        ref_row, tot_row = CHUNK // 2 - 1, CHUNK - 1
    head_of_lane = lax.broadcasted_iota(jnp.int32, (1, D_GLA_K), 1) // HEAD_K
    head_masks = [(head_of_lane == h).astype(BF16) for h in range(GLA_HEADS)]

    def stack_heads(a):
        ab = a.astype(BF16)
        return jnp.concatenate([ab * m for m in head_masks], axis=0)

    tok = lambda c: slice(c * CHUNK, (c + 1) * CHUNK)
    stk = lambda c: slice(c * stacked, (c + 1) * stacked)
    vrows = lambda c: slice(c * HEAD_V, (c + 1) * HEAD_V)

    cums = []
    for c in range(n_chunks):
        la = la_ref[tok(c), :]
        la_hi = la.astype(BF16)
        la_lo = (la - la_hi.astype(F32)).astype(BF16)
        cums.append(_dot(cum, la_hi) + _dot(cum, la_lo))

    decays = []
    for c in range(n_chunks):
        b = cums[c]
        b_ref = b[ref_row:ref_row + 1, :]
        b_tot = b[tot_row:tot_row + 1, :]
        q = q_ref[tok(c), q0:q0 + D_GLA_K] * (HEAD_K ** -0.5)
        k = k_ref[tok(c), k0:k0 + D_GLA_K]
        v = v_ref[tok(c), v0:v0 + D_GLA_V]
        qin_ref[stk(c), :] = stack_heads(q * jnp.exp(b - b_ref))
        kin_ref[tok(c), :] = (k * jnp.exp(b_ref - b)).astype(BF16)
        kdec_ref[stk(c), :] = stack_heads(k * jnp.exp(b_tot - b))
        qdec_ref[stk(c), :] = stack_heads(q * jnp.exp(b))
        v_st = jnp.concatenate([v[:, h * HEAD_V:(h + 1) * HEAD_V] for h in range(GLA_HEADS)], axis=0)
        vst_ref[stk(c), :] = v_st.astype(BF16)
        vstt_ref[vrows(c), :] = v_st.T.astype(BF16)
        decays.append(jnp.exp(b_tot))

    for c in range(n_chunks):
        k_in = kin_ref[tok(c), :]
        att = _dot_nt(qin_ref[stk(c), :], jnp.concatenate([k_in] * GLA_HEADS, axis=0))
        att_ref[stk(c), :] = jnp.where(att_mask, att, 0.0).astype(BF16)
        kv_ref[vrows(c), :] = _dot(vstt_ref[vrows(c), :], kdec_ref[stk(c), :])
        if interleave is not None:
            interleave(c)

    state = st_ref[...]
    for step in range(n_chunks):
        c = n_chunks - 1 - step if reverse else step
        sb_ref[vrows(c), :] = state.astype(BF16)
        state = state * decays[c] + kv_ref[vrows(c), :]
    st_ref[...] = state

    for c in range(n_chunks):
        o_st = _dot(att_ref[stk(c), :], vst_ref[stk(c), :]) + _dot_nt(qdec_ref[stk(c), :], sb_ref[vrows(c), :])
        o_ref[tok(c), :] = jnp.concatenate(
            [o_st[h * CHUNK:(h + 1) * CHUNK, :] for h in range(GLA_HEADS)], axis=1)


def _gla_bwd_kernel(x_ref, xnext_ref, wqk_ref, wv_ref, wlr_ref, w2_ref, b2_ref, o_ref,
                    p0_ref, la0_ref, p1_ref, la1_ref, st_ref, *gla_scratch):
    step_no = pl.program_id(0) * pl.num_programs(1) + pl.program_id(1)
    pieces = [(w_ref, c0) for w_ref in (wqk_ref, wv_ref) for c0 in range(0, w_ref.shape[1], MXU_DIM)]

    def project_piece(xb, p_ref, j):
        w_ref, c0 = pieces[j]
        p_ref[:, j * MXU_DIM:(j + 1) * MXU_DIM] = _dot(xb, w_ref[:, c0:c0 + MXU_DIM])

    @pl.when(step_no == 0)
    def _():
        xb = x_ref[0].astype(BF16)
        la0_ref[...] = _log_decay(xb, wlr_ref, w2_ref, b2_ref)
        for j in range(len(pieces)):
            project_piece(xb, p0_ref, j)

    @pl.when(pl.program_id(1) == 0)
    def _():
        st_ref[...] = jnp.zeros(st_ref.shape, F32)

    n_chunks = x_ref.shape[1] // CHUNK
    n_pieces = len(pieces)
    stride = n_chunks // n_pieces
    assert stride >= 1 and 1 + (n_pieces - 1) * stride <= n_chunks - 1

    def run(p_ref, la_ref, p_next_ref, la_next_ref):
        xnb = xnext_ref[0].astype(BF16)

        def project_next_piece(step):
            if step == 0:
                la_next_ref[...] = _log_decay(xnb, wlr_ref, w2_ref, b2_ref)
            elif (step - 1) % stride == 0 and (step - 1) // stride < n_pieces:
                project_piece(xnb, p_next_ref, (step - 1) // stride)

        _gla_tile(p_ref, p_ref, p_ref, la_ref, o_ref.at[0], st_ref, gla_scratch,
                  q0=0, k0=D_GLA_K, v0=2 * D_GLA_K, reverse=True, interleave=project_next_piece)

    parity = step_no & 1

    @pl.when(parity == 0)
    def _():
        run(p0_ref, la0_ref, p1_ref, la1_ref)

    @pl.when(parity == 1)
    def _():
        run(p1_ref, la1_ref, p0_ref, la0_ref)


def _gla_bwd_call(x, w, l):
    bsz, seq, _ = x.shape
    nt = seq // TIME_TILE
    rev = lambda b, i: (b, nt - 1 - i, 0)

    def rev_next(b, i):
        nxt = jnp.minimum(b * nt + i + 1, bsz * nt - 1)
        return (nxt // nt, nt - 1 - nxt % nt, 0)

    return pl.pallas_call(
        _gla_bwd_kernel,
        grid=(bsz, nt),
        in_specs=[
            pl.BlockSpec((1, TIME_TILE, D_MODEL), rev),
            pl.BlockSpec((1, TIME_TILE, D_MODEL), rev_next),
            _weight_spec(w["w_in"], (l,), (D_MODEL, 2 * D_GLA_K), (0, _Q0 // (2 * D_GLA_K))),
            _weight_spec(w["w_in"], (l,), (D_MODEL, D_GLA_V), (0, _V0 // D_GLA_V)),
            _weight_spec(w["w_lr"], (l,)),
            _weight_spec(w["w2_b"], (l,)),
            _weight_spec(w["gate_b"], (l, 1)),
        ],
        out_specs=pl.BlockSpec((1, TIME_TILE, D_GLA_V), rev),
        out_shape=jax.ShapeDtypeStruct((bsz, seq, D_GLA_V), F32),
        scratch_shapes=[
            pltpu.VMEM((TIME_TILE, 2 * D_GLA_K + D_GLA_V), F32),
            pltpu.VMEM((TIME_TILE, D_GLA_K), F32),
            pltpu.VMEM((TIME_TILE, 2 * D_GLA_K + D_GLA_V), F32),
            pltpu.VMEM((TIME_TILE, D_GLA_K), F32),
            pltpu.VMEM((HEAD_V, D_GLA_K), F32),
        ] + _gla_scratch(TIME_TILE),
        compiler_params=pltpu.CompilerParams(
            dimension_semantics=("arbitrary", "arbitrary"), vmem_limit_bytes=VMEM_LIMIT_BYTES),
        name="gla_bwd",
    )(x, x, w["w_in"], w["w_in"], w["w_lr"], w["w2_b"], w["gate_b"])


def _mixer_kernel(x_ref, xprev_ref, xnext_ref, ob_ref, wp_ref, wlr_ref, w2_ref, b2_ref, wm_ref,
                  convw_ref, wco_ref, gn_ref, wgo_ref, wout_ref, g_ref, b_ref,
                  o_ref, p_ref, la_ref, of_ref, mg_ref, st_ref, *gla_scratch):
    i = pl.program_id(1)
    nt = pl.num_programs(1)

    @pl.when(i == 0)
    def _():
        st_ref[...] = jnp.zeros(st_ref.shape, F32)

    x = x_ref[0]
    xb = x.astype(BF16)
    la_ref[...] = _log_decay(xb, wlr_ref, w2_ref, b2_ref)
    p_ref[...] = _dot(xb, wp_ref[...])

    n_chunks = x.shape[0] // CHUNK
    piece = 2 * D_MODEL // n_chunks
    assert piece % MXU_DIM == 0

    def merge_gate_piece(step):
        cols = slice(step * piece, (step + 1) * piece)
        mg_ref[:, cols] = _dot(xb, wm_ref[:, cols])

    _gla_tile(p_ref, p_ref, p_ref, la_ref, of_ref, st_ref, gla_scratch, q0=_Q0, k0=_K0, v0=_V0,
              reverse=False, interleave=merge_gate_piece)
    gn = gn_ref[...]
    normed = []
    for h in range(GLA_HEADS):
        sl = slice(h * HEAD_V, (h + 1) * HEAD_V)
        o_h = of_ref[:, sl] + ob_ref[0, :, sl]
        ms = jnp.mean(o_h * o_h, axis=-1, keepdims=True)
        normed.append(o_h * lax.rsqrt(ms + RMS_EPS) * gn)
    gate = p_ref[:, _G0:_G0 + D_GLA_V]
    o_gla = jnp.concatenate(normed, axis=1) * (gate * jax.nn.sigmoid(gate))
    y_gla = _dot(o_gla.astype(BF16), wgo_ref[...])

    u = p_ref[:, _GC0:_GC0 + D_CONV] * p_ref[:, _H0:_H0 + D_CONV]
    xh = jnp.concatenate([xprev_ref[0], xnext_ref[0]], axis=0).astype(BF16)
    ph = _dot(xh, wp_ref[:, 0:3 * D_CONV])
    uh = ph[:, _GC0:_GC0 + D_CONV] * ph[:, _H0:_H0 + D_CONV]
    u_before = jnp.where(i > 0, uh[SUBLANES - 1:SUBLANES, :], 0.0)
    u_after = jnp.where(i < nt - 1, uh[SUBLANES:SUBLANES + 1, :], 0.0)
    rows = u.shape[0]
    row = lax.broadcasted_iota(jnp.int32, (rows, 1), 0)
    u_m1 = jnp.where(row == 0, u_before, pltpu.roll(u, 1, axis=0))
    u_p1 = jnp.where(row == rows - 1, u_after, pltpu.roll(u, rows - 1, axis=0))
    cw = convw_ref[...]
    conv = u_m1 * cw[0:1, :] + u * cw[1:2, :] + u_p1 * cw[2:3, :]
    y_conv = _dot((p_ref[:, _GB0:_GB0 + D_CONV] * conv).astype(BF16), wco_ref[...])

    merged = (jax.nn.sigmoid(mg_ref[:, :D_MODEL]) * y_conv
              + jax.nn.sigmoid(mg_ref[:, D_MODEL:]) * y_gla)
    merged = merged.astype(BF16)
    half = rows // 2
    for r0 in (0, half):
        y = _dot(merged[r0:r0 + half, :], wout_ref[...])
        o_ref[0, r0:r0 + half, :] = _residual_layer_norm(x[r0:r0 + half, :], y, g_ref[...], b_ref[...])


def _mixer_call(x, o_b, w, l):
    bsz, seq, _ = x.shape
    nt = seq // TIME_TILE
    per8 = TIME_TILE // SUBLANES
    last8 = seq // SUBLANES - 1
    tile = lambda b, i: (b, i, 0)
    return pl.pallas_call(
        _mixer_kernel,
        grid=(bsz, nt),
        in_specs=[
            pl.BlockSpec((1, TIME_TILE, D_MODEL), tile),
            pl.BlockSpec((1, SUBLANES, D_MODEL), lambda b, i: (b, jnp.maximum(i * per8 - 1, 0), 0)),
            pl.BlockSpec((1, SUBLANES, D_MODEL), lambda b, i: (b, jnp.minimum((i + 1) * per8, last8), 0)),
            pl.BlockSpec((1, TIME_TILE, D_GLA_V), tile),
            _weight_spec(w["w_in"], (l,), (D_MODEL, D_PROJ), (0, 0)),
            _weight_spec(w["w_lr"], (l,)),
            _weight_spec(w["w2_f"], (l,)),
            _weight_spec(w["gate_b"], (l, 0)),
            _weight_spec(w["w_merge"], (l,)),
            _weight_spec(w["conv_w"], (l,)),
            _weight_spec(w["w_conv_out"], (l,)),
            _weight_spec(w["gla_norm_g"], (l,)),
            _weight_spec(w["w_gla_out"], (l,)),
            _weight_spec(w["w_mix_out"], (l,)),
            _weight_spec(w["ln_g"], (l, 1)),
            _weight_spec(w["ln_b"], (l, 1)),
        ],
        out_specs=pl.BlockSpec((1, TIME_TILE, D_MODEL), tile),
        out_shape=jax.ShapeDtypeStruct((bsz, seq, D_MODEL), F32),
        scratch_shapes=[
            pltpu.VMEM((TIME_TILE, D_PROJ), F32),
            pltpu.VMEM((TIME_TILE, D_GLA_K), F32),
            pltpu.VMEM((TIME_TILE, D_GLA_V), F32),
            pltpu.VMEM((TIME_TILE, 2 * D_MODEL), F32),
            pltpu.VMEM((HEAD_V, D_GLA_K), F32),
        ] + _gla_scratch(TIME_TILE),
        compiler_params=pltpu.CompilerParams(
            dimension_semantics=("arbitrary", "arbitrary"), vmem_limit_bytes=VMEM_LIMIT_BYTES),
        name="mixer_ln",
    )(x, x, x, o_b, w["w_in"], w["w_lr"], w["w2_f"], w["gate_b"], w["w_merge"], w["conv_w"],
      w["w_conv_out"], w["gla_norm_g"], w["w_gla_out"], w["w_mix_out"], w["ln_g"], w["ln_b"])


def _xattn_kernel(x_ref, mem_ref, wq_ref, wkv_ref, wo_ref, g_ref, b_ref, o_ref, kv_ref):
    @pl.when(pl.program_id(1) == 0)
    def _():
        kv_ref[...] = _dot(mem_ref[0].astype(BF16), wkv_ref[...]).astype(BF16)

    subs = [slice(r0, r0 + XA_ROW_SUB) for r0 in range(0, x_ref.shape[1], XA_ROW_SUB)]
    head_cols = [slice(h * XA_HEAD_DIM, (h + 1) * XA_HEAD_DIM) for h in range(XA_HEADS)]
    qs = [_dot(x_ref[0, rs, :].astype(BF16), wq_ref[...]).astype(BF16) for rs in subs]
    scores = [[_dot_nt(q[:, hc], kv_ref[:, hc]) * (XA_HEAD_DIM ** -0.5) for hc in head_cols] for q in qs]
    outs = []
    for sub_scores in scores:
        heads = []
        for h, s in enumerate(sub_scores):
            v_h = kv_ref[:, D_MODEL + h * XA_HEAD_DIM:D_MODEL + (h + 1) * XA_HEAD_DIM]
            e = jnp.exp(s - jnp.max(s, axis=-1, keepdims=True))
            p = e / jnp.sum(e, axis=-1, keepdims=True)
            heads.append(_dot(p.astype(BF16), v_h))
        outs.append(jnp.concatenate(heads, axis=1).astype(BF16))
    ys = [_dot(o, wo_ref[...]) for o in outs]
    for rs, y in zip(subs, ys):
        o_ref[0, rs, :] = _residual_layer_norm(x_ref[0, rs, :], y, g_ref[...], b_ref[...])


def _xattn_call(x, mem, w, l):
    bsz, seq, _ = x.shape
    n_mem = mem.shape[1]
    tile = lambda b, i: (b, i, 0)
    return pl.pallas_call(
        _xattn_kernel,
        grid=(bsz, seq // XA_TILE),
        in_specs=[
            pl.BlockSpec((1, XA_TILE, D_MODEL), tile),
            pl.BlockSpec((1, n_mem, D_MODEL), lambda b, i: (b, 0, 0)),
            _weight_spec(w["xa_w_q"], (l,)),
            _weight_spec(w["xa_w_kv"], (l,)),
            _weight_spec(w["xa_w_o"], (l,)),
            _weight_spec(w["ln_g"], (l, 2)),
            _weight_spec(w["ln_b"], (l, 2)),
        ],
        out_specs=pl.BlockSpec((1, XA_TILE, D_MODEL), tile),
        out_shape=jax.ShapeDtypeStruct((bsz, seq, D_MODEL), F32),
        scratch_shapes=[pltpu.VMEM((n_mem, 2 * D_MODEL), BF16)],
        compiler_params=pltpu.CompilerParams(
            dimension_semantics=("arbitrary", "arbitrary"), vmem_limit_bytes=VMEM_LIMIT_BYTES),
        name="xattn_ln",
    )(x, mem, w["xa_w_q"], w["xa_w_kv"], w["xa_w_o"], w["ln_g"], w["ln_b"])


def _prepare_weights(ffn1_w_gu, ffn1_w_down, w_mix_in, conv_w, w_conv_out, gla_gate_w2, gla_gate_b,
                     gla_norm_g, w_gla_out, w_mix_out, xa_w_q, xa_w_kv, xa_w_o, ffn2_w_gu, ffn2_w_down,
                     ln_g, ln_b):
    bf = lambda a: a.astype(BF16)
    w_in = bf(w_mix_in)
    lr_pad = LR_PAD - 2 * GATE_RANK
    w2 = bf(gla_gate_w2)
    return dict(
        ffn1_w_gu=bf(ffn1_w_gu), ffn1_w_down=bf(ffn1_w_down),
        ffn2_w_gu=bf(ffn2_w_gu), ffn2_w_down=bf(ffn2_w_down),
        w_in=w_in,
        w_lr=jnp.pad(w_in[:, :, _LR0:_MERGE0], ((0, 0), (0, 0), (0, lr_pad))),
        w2_f=jnp.pad(w2[:, 0], ((0, 0), (0, LR_PAD - GATE_RANK), (0, 0))),
        w2_b=jnp.pad(w2[:, 1], ((0, 0), (GATE_RANK, lr_pad), (0, 0))),
        gate_b=gla_gate_b[:, :, None, :],
        w_merge=w_in[:, :, _MERGE0:], conv_w=conv_w, w_conv_out=bf(w_conv_out),
        gla_norm_g=gla_norm_g[:, None, :], w_gla_out=bf(w_gla_out), w_mix_out=bf(w_mix_out),
        xa_w_q=bf(xa_w_q), xa_w_kv=bf(xa_w_kv), xa_w_o=bf(xa_w_o),
        ln_g=ln_g[:, :, None, :], ln_b=ln_b[:, :, None, :],
    )


def _trunk(x, mem, w):
    bsz, seq, d = x.shape
    for l in range(DEPTH):
        x = _ffn_call(x.reshape(bsz * seq, d), w["ffn1_w_gu"], w["ffn1_w_down"], w, l, 0).reshape(bsz, seq, d)
        o_b = _gla_bwd_call(x, w, l)
        x = _mixer_call(x, o_b, w, l)
        x = _xattn_call(x, mem, w, l)
        x = _ffn_call(x.reshape(bsz * seq, d), w["ffn2_w_gu"], w["ffn2_w_down"], w, l, 3).reshape(bsz, seq, d)
    return x


def kernel(x_prompt, x_sample, mem_prompt, mem_sample, ffn1_w_gu, ffn1_w_down, w_mix_in, conv_w, w_conv_out, gla_gate_w2, gla_gate_b, gla_norm_g, w_gla_out, w_mix_out, xa_w_q, xa_w_kv, xa_w_o, ffn2_w_gu, ffn2_w_down, ln_g, ln_b):
    w = _prepare_weights(ffn1_w_gu, ffn1_w_down, w_mix_in, conv_w, w_conv_out, gla_gate_w2, gla_gate_b,
                         gla_norm_g, w_gla_out, w_mix_out, xa_w_q, xa_w_kv, xa_w_o, ffn2_w_gu,
                         ffn2_w_down, ln_g, ln_b)
    return (_trunk(x_prompt, mem_prompt, w), _trunk(x_sample, mem_sample, w))
```

```python
import functools

import jax
import jax.numpy as jnp
from jax import lax
from jax.experimental import pallas as pl
from jax.experimental.pallas import tpu as pltpu

F32 = jnp.float32
BF16 = jnp.bfloat16

D_MODEL = 1024
DEPTH = 4
D_FF = 2816
D_CONV = D_MODEL // 2
GLA_HEADS = 4
D_GLA_V = D_MODEL // 2
HEAD_V = D_GLA_V // GLA_HEADS
D_GLA_K = D_GLA_V // 2
HEAD_K = D_GLA_K // GLA_HEADS
GATE_RANK = 16
GATE_TEMP = 16.0
CHUNK = 64
XA_HEADS = 4
XA_HEAD_DIM = D_MODEL // XA_HEADS
LN_EPS = 1e-5
RMS_EPS = 1e-6
DN_ALPHA = (2 * DEPTH) ** 0.25

SUBLANES = 8
LANES = 128
VMEM_LIMIT_BYTES = 56 * 1024 * 1024

TOKEN_TILE = 1024
TIME_TILE = 512
MXU_DIM = 256
XA_TILE = 1024
XA_ROW_SUB = 512
FFN_ROW_SUBS = (512, 512)
assert sum(FFN_ROW_SUBS) == TOKEN_TILE
FF_CHUNKS = (6 * MXU_DIM, 5 * MXU_DIM)
assert sum(FF_CHUNKS) == D_FF
LR_PAD = LANES

_H0, _GB0, _GC0 = 0, D_CONV, 2 * D_CONV
_Q0 = 3 * D_CONV
_K0 = _Q0 + D_GLA_K
_V0 = _K0 + D_GLA_K
_G0 = _V0 + D_GLA_V
D_PROJ = _G0 + D_GLA_V
_LR0 = D_PROJ
_MERGE0 = _LR0 + 2 * GATE_RANK
assert _Q0 % (2 * D_GLA_K) == 0 and _V0 % D_GLA_V == 0


def _dot(a, b):
    return jnp.dot(a, b, preferred_element_type=F32)


def _dot_nt(a, b):
    return lax.dot_general(a, b, (((1,), (1,)), ((), ())), preferred_element_type=F32)


def _residual_layer_norm(x, y, g, b):
    z = DN_ALPHA * x + y
    mu = jnp.mean(z, axis=-1, keepdims=True)
    zc = z - mu
    var = jnp.mean(zc * zc, axis=-1, keepdims=True)
    return zc * lax.rsqrt(var + LN_EPS) * g + b


def _weight_spec(arr, lead, block=None, index=None):
    lead = tuple(lead)
    tail = tuple(arr.shape[len(lead):]) if block is None else tuple(block)
    pos = (0,) * len(tail) if index is None else tuple(index)
    return pl.BlockSpec((None,) * len(lead) + tail, lambda *_: lead + pos, pipeline_mode=pl.Buffered(1))


def _ffn_kernel(x_ref, wgu_ref, wd_ref, g_ref, b_ref, o_ref):
    r0 = 0
    for sub in FFN_ROW_SUBS:
        x = x_ref[r0:r0 + sub, :]
        xb = x.astype(BF16)
        acc = None
        c0 = 0
        for width in FF_CHUNKS:
            gate = _dot(xb, wgu_ref[:, c0:c0 + width])
            up = _dot(xb, wgu_ref[:, D_FF + c0:D_FF + c0 + width])
            act = (gate * jax.nn.sigmoid(gate)) * up
            part = _dot(act.astype(BF16), wd_ref[c0:c0 + width, :])
            acc = part if acc is None else acc + part
            c0 += width
        o_ref[r0:r0 + sub, :] = _residual_layer_norm(x, 0.5 * acc, g_ref[...], b_ref[...])
        r0 += sub


def _ffn_call(x2d, w_gu, w_down, w, l, ln_idx):
    n = x2d.shape[0]
    assert n % TOKEN_TILE == 0
    return pl.pallas_call(
        _ffn_kernel,
        grid=(n // TOKEN_TILE,),
        in_specs=[
            pl.BlockSpec((TOKEN_TILE, D_MODEL), lambda i: (i, 0)),
            _weight_spec(w_gu, (l,)),
            _weight_spec(w_down, (l,)),
            _weight_spec(w["ln_g"], (l, ln_idx)),
            _weight_spec(w["ln_b"], (l, ln_idx)),
        ],
        out_specs=pl.BlockSpec((TOKEN_TILE, D_MODEL), lambda i: (i, 0)),
        out_shape=jax.ShapeDtypeStruct((n, D_MODEL), F32),
        compiler_params=pltpu.CompilerParams(
            dimension_semantics=("arbitrary",), vmem_limit_bytes=VMEM_LIMIT_BYTES),
        name="ffn_ln",
    )(x2d, w_gu, w_down, w["ln_g"], w["ln_b"])


def _log_decay(xb, wlr_ref, w2_ref, b2_ref):
    lr = _dot(xb, wlr_ref[...])
    z = _dot(lr.astype(BF16), w2_ref[...]) + b2_ref[...]
    return jax.nn.log_sigmoid(z) / GATE_TEMP


def _gla_scratch(rows):
    n = rows // CHUNK
    stacked = GLA_HEADS * CHUNK
    return [
        pltpu.VMEM((n * stacked, D_GLA_K), BF16),
        pltpu.VMEM((n * stacked, D_GLA_K), BF16),
        pltpu.VMEM((n * stacked, D_GLA_K), BF16),
        pltpu.VMEM((rows, D_GLA_K), BF16),
        pltpu.VMEM((n * stacked, HEAD_V), BF16),
        pltpu.VMEM((n * HEAD_V, stacked), BF16),
        pltpu.VMEM((n * stacked, stacked), BF16),
        pltpu.VMEM((n * HEAD_V, D_GLA_K), F32),
    ]


def _gla_tile(q_ref, k_ref, v_ref, la_ref, o_ref, st_ref, scratch, *, q0, k0, v0, reverse,
              interleave=None):
    qin_ref, qdec_ref, kdec_ref, kin_ref, vst_ref, vstt_ref, att_ref, kv_ref = scratch
    rows = q_ref.shape[0]
    n_chunks = rows // CHUNK
    ri = lax.broadcasted_iota(jnp.int32, (CHUNK, CHUNK), 0)
    ci = lax.broadcasted_iota(jnp.int32, (CHUNK, CHUNK), 1)
    stacked = GLA_HEADS * CHUNK
    sr = lax.broadcasted_iota(jnp.int32, (stacked, stacked), 0)
    sc = lax.broadcasted_iota(jnp.int32, (stacked, stacked), 1)
    same_head = (sr // CHUNK) == (sc // CHUNK)
    if reverse:
        cum = (ci >= ri).astype(BF16)
        att_mask = same_head & ((sc % CHUNK) > (sr % CHUNK))
        ref_row, tot_row = CHUNK // 2, 0
    else:
        cum = (ci <= ri).astype(BF16)
        att_mask = same_head & ((sc % CHUNK) <= (sr % CHUNK))
        ref_row, tot_row = CHUNK // 2 - 1, CHUNK - 1
    head_of_lane = lax.broadcasted_iota(jnp.int32, (1, D_GLA_K), 1) // HEAD_K
    head_masks = [(head_of_lane == h).astype(BF16) for h in range(GLA_HEADS)]

    def stack_heads(a):
        ab = a.astype(BF16)
        return jnp.concatenate([ab * m for m in head_masks], axis=0)

    tok = lambda c: slice(c * CHUNK, (c + 1) * CHUNK)
    stk = lambda c: slice(c * stacked, (c + 1) * stacked)
    vrows = lambda c: slice(c * HEAD_V, (c + 1) * HEAD_V)

    cums = []
    for c in range(n_chunks):
        la = la_ref[tok(c), :]
        la_hi = la.astype(BF16)
        la_lo = (la - la_hi.astype(F32)).astype(BF16)
        cums.append(_dot(cum, la_hi) + _dot(cum, la_lo))

    decays = []
    for c in range(n_chunks):
        b = cums[c]
        b_ref = b[ref_row:ref_row + 1, :]
        b_tot = b[tot_row:tot_row + 1, :]
        q = q_ref[tok(c), q0:q0 + D_GLA_K] * (HEAD_K ** -0.5)
        k = k_ref[tok(c), k0:k0 + D_GLA_K]
        v = v_ref[tok(c), v0:v0 + D_GLA_V]
        qin_ref[stk(c), :] = stack_heads(q * jnp.exp(b - b_ref))
        kin_ref[tok(c), :] = (k * jnp.exp(b_ref - b)).astype(BF16)
        kdec_ref[stk(c), :] = stack_heads(k * jnp.exp(b_tot - b))
        qdec_ref[stk(c), :] = stack_heads(q * jnp.exp(b))
        v_st = jnp.concatenate([v[:, h * HEAD_V:(h + 1) * HEAD_V] for h in range(GLA_HEADS)], axis=0)
        vst_ref[stk(c), :] = v_st.astype(BF16)
        vstt_ref[vrows(c), :] = v_st.T.astype(BF16)
        decays.append(jnp.exp(b_tot))
        if interleave is not None:
            interleave("operands", c)

    for c in range(n_chunks):
        k_in = kin_ref[tok(c), :]
        att = _dot_nt(qin_ref[stk(c), :], jnp.concatenate([k_in] * GLA_HEADS, axis=0))
        att_ref[stk(c), :] = jnp.where(att_mask, att, 0.0).astype(BF16)
        kv_ref[vrows(c), :] = _dot(vstt_ref[vrows(c), :], kdec_ref[stk(c), :])
        if interleave is not None:
            interleave("scores", c)

    state = st_ref[...]
    for step in range(n_chunks):
        c = n_chunks - 1 - step if reverse else step
        o_st = _dot(att_ref[stk(c), :], vst_ref[stk(c), :]) + _dot_nt(qdec_ref[stk(c), :], state.astype(BF16))
        o_ref[tok(c), :] = jnp.concatenate(
            [o_st[h * CHUNK:(h + 1) * CHUNK, :] for h in range(GLA_HEADS)], axis=1)
        state = state * decays[c] + kv_ref[vrows(c), :]
    st_ref[...] = state


def _gla_bwd_kernel(x_ref, xnext_ref, wqk_ref, wv_ref, wlr_ref, w2_ref, b2_ref, o_ref,
                    p0_ref, la0_ref, p1_ref, la1_ref, st_ref, *gla_scratch):
    step_no = pl.program_id(0) * pl.num_programs(1) + pl.program_id(1)
    pieces = [(w_ref, c0) for w_ref in (wqk_ref, wv_ref) for c0 in range(0, w_ref.shape[1], MXU_DIM)]

    def project_piece(xb, p_ref, j):
        w_ref, c0 = pieces[j]
        p_ref[:, j * MXU_DIM:(j + 1) * MXU_DIM] = _dot(xb, w_ref[:, c0:c0 + MXU_DIM])

    @pl.when(step_no == 0)
    def _():
        xb = x_ref[0].astype(BF16)
        la0_ref[...] = _log_decay(xb, wlr_ref, w2_ref, b2_ref)
        for j in range(len(pieces)):
            project_piece(xb, p0_ref, j)

    @pl.when(pl.program_id(1) == 0)
    def _():
        st_ref[...] = jnp.zeros(st_ref.shape, F32)

    n_chunks = x_ref.shape[1] // CHUNK
    n_pieces = len(pieces)
    stride = n_chunks // n_pieces
    assert stride >= 1 and 1 + (n_pieces - 1) * stride <= n_chunks - 1

    def run(p_ref, la_ref, p_next_ref, la_next_ref):
        xnb = xnext_ref[0].astype(BF16)

        def project_next_piece(stage, step):
            if stage != "operands":
                return
            if step == 0:
                la_next_ref[...] = _log_decay(xnb, wlr_ref, w2_ref, b2_ref)
            elif (step - 1) % stride == 0 and (step - 1) // stride < n_pieces:
                project_piece(xnb, p_next_ref, (step - 1) // stride)

        _gla_tile(p_ref, p_ref, p_ref, la_ref, o_ref.at[0], st_ref, gla_scratch,
                  q0=0, k0=D_GLA_K, v0=2 * D_GLA_K, reverse=True, interleave=project_next_piece)

    parity = step_no & 1

    @pl.when(parity == 0)
    def _():
        run(p0_ref, la0_ref, p1_ref, la1_ref)

    @pl.when(parity == 1)
    def _():
        run(p1_ref, la1_ref, p0_ref, la0_ref)


def _gla_bwd_call(x, w, l):
    bsz, seq, _ = x.shape
    nt = seq // TIME_TILE
    rev = lambda b, i: (b, nt - 1 - i, 0)

    def rev_next(b, i):
        nxt = jnp.minimum(b * nt + i + 1, bsz * nt - 1)
        return (nxt // nt, nt - 1 - nxt % nt, 0)

    return pl.pallas_call(
        _gla_bwd_kernel,
        grid=(bsz, nt),
        in_specs=[
            pl.BlockSpec((1, TIME_TILE, D_MODEL), rev),
            pl.BlockSpec((1, TIME_TILE, D_MODEL), rev_next),
            _weight_spec(w["w_in"], (l,), (D_MODEL, 2 * D_GLA_K), (0, _Q0 // (2 * D_GLA_K))),
            _weight_spec(w["w_in"], (l,), (D_MODEL, D_GLA_V), (0, _V0 // D_GLA_V)),
            _weight_spec(w["w_lr"], (l,)),
            _weight_spec(w["w2_b"], (l,)),
            _weight_spec(w["gate_b"], (l, 1)),
        ],
        out_specs=pl.BlockSpec((1, TIME_TILE, D_GLA_V), rev),
        out_shape=jax.ShapeDtypeStruct((bsz, seq, D_GLA_V), F32),
        scratch_shapes=[
            pltpu.VMEM((TIME_TILE, 2 * D_GLA_K + D_GLA_V), F32),
            pltpu.VMEM((TIME_TILE, D_GLA_K), F32),
            pltpu.VMEM((TIME_TILE, 2 * D_GLA_K + D_GLA_V), F32),
            pltpu.VMEM((TIME_TILE, D_GLA_K), F32),
            pltpu.VMEM((HEAD_V, D_GLA_K), F32),
        ] + _gla_scratch(TIME_TILE),
        compiler_params=pltpu.CompilerParams(
            dimension_semantics=("arbitrary", "arbitrary"), vmem_limit_bytes=VMEM_LIMIT_BYTES),
        name="gla_bwd",
    )(x, x, w["w_in"], w["w_in"], w["w_lr"], w["w2_b"], w["gate_b"])


def _mixer_kernel(x_ref, xprev_ref, xnext_ref, ob_ref, wp_ref, wlr_ref, w2_ref, b2_ref, wm_ref,
                  convw_ref, wco_ref, gn_ref, wgo_ref, wout_ref, g_ref, b_ref,
                  o_ref, p_ref, la_ref, of_ref, mg_ref, yconv_ref, st_ref, *gla_scratch):
    i = pl.program_id(1)
    nt = pl.num_programs(1)

    @pl.when(i == 0)
    def _():
        st_ref[...] = jnp.zeros(st_ref.shape, F32)

    x = x_ref[0]
    xb = x.astype(BF16)
    rows = x.shape[0]
    x_edges = jnp.concatenate([xprev_ref[0], xnext_ref[0]], axis=0).astype(BF16)
    p_conv = _dot(jnp.concatenate([xb, x_edges], axis=0), wp_ref[:, 0:_Q0])
    p_ref[:, 0:_Q0] = p_conv[0:rows, :]
    p_edges = p_conv[rows:, :]
    la_ref[...] = _log_decay(xb, wlr_ref, w2_ref, b2_ref)
    p_ref[:, _Q0:] = _dot(xb, wp_ref[:, _Q0:])

    n_chunks = x.shape[0] // CHUNK
    piece = 2 * D_MODEL // n_chunks
    assert piece % MXU_DIM == 0

    def merge_gate_piece(step):
        cols = slice(step * piece, (step + 1) * piece)
        mg_ref[:, cols] = _dot(xb, wm_ref[:, cols])

    def conv_branch():
        u = p_ref[:, _GC0:_GC0 + D_CONV] * p_ref[:, _H0:_H0 + D_CONV]
        uh = p_edges[:, _GC0:_GC0 + D_CONV] * p_edges[:, _H0:_H0 + D_CONV]
        u_before = jnp.where(i > 0, uh[SUBLANES - 1:SUBLANES, :], 0.0)
        u_after = jnp.where(i < nt - 1, uh[SUBLANES:SUBLANES + 1, :], 0.0)
        n = u.shape[0]
        row = lax.broadcasted_iota(jnp.int32, (n, 1), 0)
        u_m1 = jnp.where(row == 0, u_before, pltpu.roll(u, 1, axis=0))
        u_p1 = jnp.where(row == n - 1, u_after, pltpu.roll(u, n - 1, axis=0))
        cw = convw_ref[...]
        conv = u_m1 * cw[0:1, :] + u * cw[1:2, :] + u_p1 * cw[2:3, :]
        yconv_ref[...] = _dot((p_ref[:, _GB0:_GB0 + D_CONV] * conv).astype(BF16), wco_ref[...])

    def dense_filler(stage, c):
        if stage == "operands":
            merge_gate_piece(c)
        elif c == 0:
            conv_branch()

    _gla_tile(p_ref, p_ref, p_ref, la_ref, of_ref, st_ref, gla_scratch, q0=_Q0, k0=_K0, v0=_V0,
              reverse=False, interleave=dense_filler)

    halves = [slice(0, rows // 2), slice(rows // 2, rows)]
    gn = gn_ref[...]
    o_glas = []
    for rs in halves:
        normed = []
        for h in range(GLA_HEADS):
            sl = slice(h * HEAD_V, (h + 1) * HEAD_V)
            o_h = of_ref[rs, sl] + ob_ref[0, rs, sl]
            ms = jnp.mean(o_h * o_h, axis=-1, keepdims=True)
            normed.append(o_h * lax.rsqrt(ms + RMS_EPS) * gn)
        gate = p_ref[rs, _G0:_G0 + D_GLA_V]
        o_glas.append((jnp.concatenate(normed, axis=1) * (gate * jax.nn.sigmoid(gate))).astype(BF16))
    y_glas = [_dot(o_gla, wgo_ref[...]) for o_gla in o_glas]
    mergeds = [(jax.nn.sigmoid(mg_ref[rs, :D_MODEL]) * yconv_ref[rs, :]
                + jax.nn.sigmoid(mg_ref[rs, D_MODEL:]) * y_gla).astype(BF16)
               for rs, y_gla in zip(halves, y_glas)]
    ys = [_dot(merged, wout_ref[...]) for merged in mergeds]
    for rs, y in zip(halves, ys):
        o_ref[0, rs, :] = _residual_layer_norm(x_ref[0, rs, :], y, g_ref[...], b_ref[...])


def _mixer_call(x, o_b, w, l):
    bsz, seq, _ = x.shape
    nt = seq // TIME_TILE
    per8 = TIME_TILE // SUBLANES
    last8 = seq // SUBLANES - 1
    tile = lambda b, i: (b, i, 0)
    return pl.pallas_call(
        _mixer_kernel,
        grid=(bsz, nt),
        in_specs=[
            pl.BlockSpec((1, TIME_TILE, D_MODEL), tile),
            pl.BlockSpec((1, SUBLANES, D_MODEL), lambda b, i: (b, jnp.maximum(i * per8 - 1, 0), 0)),
            pl.BlockSpec((1, SUBLANES, D_MODEL), lambda b, i: (b, jnp.minimum((i + 1) * per8, last8), 0)),
            pl.BlockSpec((1, TIME_TILE, D_GLA_V), tile),
            _weight_spec(w["w_in"], (l,), (D_MODEL, D_PROJ), (0, 0)),
            _weight_spec(w["w_lr"], (l,)),
            _weight_spec(w["w2_f"], (l,)),
            _weight_spec(w["gate_b"], (l, 0)),
            _weight_spec(w["w_merge"], (l,)),
            _weight_spec(w["conv_w"], (l,)),
            _weight_spec(w["w_conv_out"], (l,)),
            _weight_spec(w["gla_norm_g"], (l,)),
            _weight_spec(w["w_gla_out"], (l,)),
            _weight_spec(w["w_mix_out"], (l,)),
            _weight_spec(w["ln_g"], (l, 1)),
            _weight_spec(w["ln_b"], (l, 1)),
        ],
        out_specs=pl.BlockSpec((1, TIME_TILE, D_MODEL), tile),
        out_shape=jax.ShapeDtypeStruct((bsz, seq, D_MODEL), F32),
        scratch_shapes=[
            pltpu.VMEM((TIME_TILE, D_PROJ), F32),
            pltpu.VMEM((TIME_TILE, D_GLA_K), F32),
            pltpu.VMEM((TIME_TILE, D_GLA_V), F32),
            pltpu.VMEM((TIME_TILE, 2 * D_MODEL), F32),
            pltpu.VMEM((TIME_TILE, D_MODEL), F32),
            pltpu.VMEM((HEAD_V, D_GLA_K), F32),
        ] + _gla_scratch(TIME_TILE),
        compiler_params=pltpu.CompilerParams(
            dimension_semantics=("arbitrary", "arbitrary"), vmem_limit_bytes=VMEM_LIMIT_BYTES),
        name="mixer_ln",
    )(x, x, x, o_b, w["w_in"], w["w_lr"], w["w2_f"], w["gate_b"], w["w_merge"], w["conv_w"],
      w["w_conv_out"], w["gla_norm_g"], w["w_gla_out"], w["w_mix_out"], w["ln_g"], w["ln_b"])


def _xattn_kernel(x_ref, mem_ref, wq_ref, wkv_ref, wo_ref, g_ref, b_ref, o_ref, kv_ref):
    @pl.when(pl.program_id(1) == 0)
    def _():
        kv_ref[...] = _dot(mem_ref[0].astype(BF16), wkv_ref[...]).astype(BF16)

    subs = [slice(r0, r0 + XA_ROW_SUB) for r0 in range(0, x_ref.shape[1], XA_ROW_SUB)]
    head_cols = [slice(h * XA_HEAD_DIM, (h + 1) * XA_HEAD_DIM) for h in range(XA_HEADS)]
    qs = [_dot(x_ref[0, rs, :].astype(BF16), wq_ref[...]).astype(BF16) for rs in subs]
    scores = [[_dot_nt(q[:, hc], kv_ref[:, hc]) * (XA_HEAD_DIM ** -0.5) for hc in head_cols] for q in qs]
    outs = []
    for sub_scores in scores:
        heads = []
        for h, s in enumerate(sub_scores):
            v_h = kv_ref[:, D_MODEL + h * XA_HEAD_DIM:D_MODEL + (h + 1) * XA_HEAD_DIM]
            e = jnp.exp(s - jnp.max(s, axis=-1, keepdims=True))
            p = e / jnp.sum(e, axis=-1, keepdims=True)
            heads.append(_dot(p.astype(BF16), v_h))
        outs.append(jnp.concatenate(heads, axis=1).astype(BF16))
    ys = [_dot(o, wo_ref[...]) for o in outs]
    for rs, y in zip(subs, ys):
        o_ref[0, rs, :] = _residual_layer_norm(x_ref[0, rs, :], y, g_ref[...], b_ref[...])


def _xattn_call(x, mem, w, l):
    bsz, seq, _ = x.shape
    n_mem = mem.shape[1]
    tile = lambda b, i: (b, i, 0)
    return pl.pallas_call(
        _xattn_kernel,
        grid=(bsz, seq // XA_TILE),
        in_specs=[
            pl.BlockSpec((1, XA_TILE, D_MODEL), tile),
            pl.BlockSpec((1, n_mem, D_MODEL), lambda b, i: (b, 0, 0)),
            _weight_spec(w["xa_w_q"], (l,)),
            _weight_spec(w["xa_w_kv"], (l,)),
            _weight_spec(w["xa_w_o"], (l,)),
            _weight_spec(w["ln_g"], (l, 2)),
            _weight_spec(w["ln_b"], (l, 2)),
        ],
        out_specs=pl.BlockSpec((1, XA_TILE, D_MODEL), tile),
        out_shape=jax.ShapeDtypeStruct((bsz, seq, D_MODEL), F32),
        scratch_shapes=[pltpu.VMEM((n_mem, 2 * D_MODEL), BF16)],
        compiler_params=pltpu.CompilerParams(
            dimension_semantics=("arbitrary", "arbitrary"), vmem_limit_bytes=VMEM_LIMIT_BYTES),
        name="xattn_ln",
    )(x, mem, w["xa_w_q"], w["xa_w_kv"], w["xa_w_o"], w["ln_g"], w["ln_b"])


def _prepare_weights(ffn1_w_gu, ffn1_w_down, w_mix_in, conv_w, w_conv_out, gla_gate_w2, gla_gate_b,
                     gla_norm_g, w_gla_out, w_mix_out, xa_w_q, xa_w_kv, xa_w_o, ffn2_w_gu, ffn2_w_down,
                     ln_g, ln_b):
    bf = lambda a: a.astype(BF16)
    w_in = bf(w_mix_in)
    lr_pad = LR_PAD - 2 * GATE_RANK
    w2 = bf(gla_gate_w2)
    return dict(
        ffn1_w_gu=bf(ffn1_w_gu), ffn1_w_down=bf(ffn1_w_down),
        ffn2_w_gu=bf(ffn2_w_gu), ffn2_w_down=bf(ffn2_w_down),
        w_in=w_in,
        w_lr=jnp.pad(w_in[:, :, _LR0:_MERGE0], ((0, 0), (0, 0), (0, lr_pad))),
        w2_f=jnp.pad(w2[:, 0], ((0, 0), (0, LR_PAD - GATE_RANK), (0, 0))),
        w2_b=jnp.pad(w2[:, 1], ((0, 0), (GATE_RANK, lr_pad), (0, 0))),
        gate_b=gla_gate_b[:, :, None, :],
        w_merge=w_in[:, :, _MERGE0:], conv_w=conv_w, w_conv_out=bf(w_conv_out),
        gla_norm_g=gla_norm_g[:, None, :], w_gla_out=bf(w_gla_out), w_mix_out=bf(w_mix_out),
        xa_w_q=bf(xa_w_q), xa_w_kv=bf(xa_w_kv), xa_w_o=bf(xa_w_o),
        ln_g=ln_g[:, :, None, :], ln_b=ln_b[:, :, None, :],
    )


def _trunk(x, mem, w):
    bsz, seq, d = x.shape
    for l in range(DEPTH):
        x = _ffn_call(x.reshape(bsz * seq, d), w["ffn1_w_gu"], w["ffn1_w_down"], w, l, 0).reshape(bsz, seq, d)
        o_b = _gla_bwd_call(x, w, l)
        x = _mixer_call(x, o_b, w, l)
        x = _xattn_call(x, mem, w, l)
        x = _ffn_call(x.reshape(bsz * seq, d), w["ffn2_w_gu"], w["ffn2_w_down"], w, l, 3).reshape(bsz, seq, d)
    return x


def kernel(x_prompt, x_sample, mem_prompt, mem_sample, ffn1_w_gu, ffn1_w_down, w_mix_in, conv_w, w_conv_out, gla_gate_w2, gla_gate_b, gla_norm_g, w_gla_out, w_mix_out, xa_w_q, xa_w_kv, xa_w_o, ffn2_w_gu, ffn2_w_down, ln_g, ln_b):
    w = _prepare_weights(ffn1_w_gu, ffn1_w_down, w_mix_in, conv_w, w_conv_out, gla_gate_w2, gla_gate_b,
                         gla_norm_g, w_gla_out, w_mix_out, xa_w_q, xa_w_kv, xa_w_o, ffn2_w_gu,
                         ffn2_w_down, ln_g, ln_b)
    return (_trunk(x_prompt, mem_prompt, w), _trunk(x_sample, mem_sample, w))
```

```python
import functools

import jax
import jax.numpy as jnp
from jax import lax
from jax.experimental import pallas as pl
from jax.experimental.pallas import tpu as pltpu

F32 = jnp.float32
BF16 = jnp.bfloat16

D_MODEL = 1024
DEPTH = 4
D_FF = 2816
D_CONV = D_MODEL // 2
GLA_HEADS = 4
D_GLA_V = D_MODEL // 2
HEAD_V = D_GLA_V // GLA_HEADS
D_GLA_K = D_GLA_V // 2
HEAD_K = D_GLA_K // GLA_HEADS
GATE_RANK = 16
GATE_TEMP = 16.0
CHUNK = 64
XA_HEADS = 4
XA_HEAD_DIM = D_MODEL // XA_HEADS
LN_EPS = 1e-5
RMS_EPS = 1e-6
DN_ALPHA = (2 * DEPTH) ** 0.25

SUBLANES = 8
LANES = 128
VMEM_LIMIT_BYTES = 56 * 1024 * 1024

TOKEN_TILE = 2048
TIME_TILE = 512
MXU_DIM = 256
XA_TILE = 1024
XA_ROW_SUB = 512
FFN_ROW_SUBS = (512, 512, 512, 512)
assert sum(FFN_ROW_SUBS) == TOKEN_TILE
FF_CHUNKS = (6 * MXU_DIM, 5 * MXU_DIM)
assert sum(FF_CHUNKS) == D_FF
LR_PAD = LANES

_H0, _GB0, _GC0 = 0, D_CONV, 2 * D_CONV
_Q0 = 3 * D_CONV
_K0 = _Q0 + D_GLA_K
_V0 = _K0 + D_GLA_K
_G0 = _V0 + D_GLA_V
D_QKV = 2 * D_GLA_K + D_GLA_V
_LR0 = _G0 + D_GLA_V
_MERGE0 = _LR0 + 2 * GATE_RANK
assert _Q0 % (2 * D_GLA_K) == 0 and _V0 % D_GLA_V == 0 and _G0 % D_GLA_V == 0


def _dot(a, b):
    return jnp.dot(a, b, preferred_element_type=F32)


def _dot_nt(a, b):
    return lax.dot_general(a, b, (((1,), (1,)), ((), ())), preferred_element_type=F32)


def _residual_layer_norm(x, y, g, b):
    z = DN_ALPHA * x + y
    mu = jnp.mean(z, axis=-1, keepdims=True)
    zc = z - mu
    var = jnp.mean(zc * zc, axis=-1, keepdims=True)
    return zc * lax.rsqrt(var + LN_EPS) * g + b


def _weight_spec(arr, lead, block=None, index=None):
    lead = tuple(lead)
    tail = tuple(arr.shape[len(lead):]) if block is None else tuple(block)
    pos = (0,) * len(tail) if index is None else tuple(index)
    return pl.BlockSpec((None,) * len(lead) + tail, lambda *_: lead + pos, pipeline_mode=pl.Buffered(1))


def _ffn_kernel(x_ref, wgu_ref, wd_ref, g_ref, b_ref, o_ref):
    r0 = 0
    for sub in FFN_ROW_SUBS:
        x = x_ref[r0:r0 + sub, :]
        xb = x.astype(BF16)
        acc = None
        c0 = 0
        for width in FF_CHUNKS:
            gate = _dot(xb, wgu_ref[:, c0:c0 + width])
            up = _dot(xb, wgu_ref[:, D_FF + c0:D_FF + c0 + width])
            act = (gate * jax.nn.sigmoid(gate)) * up
            part = _dot(act.astype(BF16), wd_ref[c0:c0 + width, :])
            acc = part if acc is None else acc + part
            c0 += width
        o_ref[r0:r0 + sub, :] = _residual_layer_norm(x, 0.5 * acc, g_ref[...], b_ref[...])
        r0 += sub


def _ffn_call(x2d, w_gu, w_down, w, l, ln_idx):
    n = x2d.shape[0]
    assert n % TOKEN_TILE == 0
    return pl.pallas_call(
        _ffn_kernel,
        grid=(n // TOKEN_TILE,),
        in_specs=[
            pl.BlockSpec((TOKEN_TILE, D_MODEL), lambda i: (i, 0)),
            _weight_spec(w_gu, (l,)),
            _weight_spec(w_down, (l,)),
            _weight_spec(w["ln_g"], (l, ln_idx)),
            _weight_spec(w["ln_b"], (l, ln_idx)),
        ],
        out_specs=pl.BlockSpec((TOKEN_TILE, D_MODEL), lambda i: (i, 0)),
        out_shape=jax.ShapeDtypeStruct((n, D_MODEL), F32),
        compiler_params=pltpu.CompilerParams(
            dimension_semantics=("arbitrary",), vmem_limit_bytes=VMEM_LIMIT_BYTES),
        name="ffn_ln",
    )(x2d, w_gu, w_down, w["ln_g"], w["ln_b"])


def _log_decay(xb, wlr_ref, w2_ref, b2_ref):
    lr = _dot(xb, wlr_ref[...])
    z = _dot(lr.astype(BF16), w2_ref[...]) + b2_ref[...]
    return jax.nn.log_sigmoid(z) / GATE_TEMP


def _gla_scratch(rows):
    n = rows // CHUNK
    stacked = GLA_HEADS * CHUNK
    return [
        pltpu.VMEM((n * stacked, D_GLA_K), BF16),
        pltpu.VMEM((n * stacked, D_GLA_K), BF16),
        pltpu.VMEM((n * stacked, D_GLA_K), BF16),
        pltpu.VMEM((rows, D_GLA_K), BF16),
        pltpu.VMEM((n * stacked, HEAD_V), BF16),
        pltpu.VMEM((n * HEAD_V, stacked), BF16),
        pltpu.VMEM((n * stacked, stacked), BF16),
        pltpu.VMEM((n * HEAD_V, D_GLA_K), F32),
    ]


def _gla_tile(q_ref, k_ref, v_ref, la_ref, o_ref, st_ref, scratch, *, q0, k0, v0, la0, reverse,
              interleave=None):
    qin_ref, qdec_ref, kdec_ref, kin_ref, vst_ref, vstt_ref, att_ref, kv_ref = scratch
    rows = q_ref.shape[0]
    n_chunks = rows // CHUNK
    ri = lax.broadcasted_iota(jnp.int32, (CHUNK, CHUNK), 0)
    ci = lax.broadcasted_iota(jnp.int32, (CHUNK, CHUNK), 1)
    stacked = GLA_HEADS * CHUNK
    sr = lax.broadcasted_iota(jnp.int32, (stacked, stacked), 0)
    sc = lax.broadcasted_iota(jnp.int32, (stacked, stacked), 1)
    same_head = (sr // CHUNK) == (sc // CHUNK)
    if reverse:
        cum = (ci >= ri).astype(BF16)
        att_mask = same_head & ((sc % CHUNK) > (sr % CHUNK))
        ref_row, tot_row = CHUNK // 2, 0
    else:
        cum = (ci <= ri).astype(BF16)
        att_mask = same_head & ((sc % CHUNK) <= (sr % CHUNK))
        ref_row, tot_row = CHUNK // 2 - 1, CHUNK - 1
    head_of_lane = lax.broadcasted_iota(jnp.int32, (1, D_GLA_K), 1) // HEAD_K
    head_masks = [(head_of_lane == h).astype(BF16) for h in range(GLA_HEADS)]

    def stack_heads(a):
        ab = a.astype(BF16)
        return jnp.concatenate([ab * m for m in head_masks], axis=0)

    tok = lambda c: slice(c * CHUNK, (c + 1) * CHUNK)
    stk = lambda c: slice(c * stacked, (c + 1) * stacked)
    vrows = lambda c: slice(c * HEAD_V, (c + 1) * HEAD_V)

    cums = []
    for c in range(n_chunks):
        la = la_ref[tok(c), la0:la0 + D_GLA_K]
        la_hi = la.astype(BF16)
        la_lo = (la - la_hi.astype(F32)).astype(BF16)
        cums.append(_dot(cum, la_hi) + _dot(cum, la_lo))

    decays = []
    for c in range(n_chunks):
        b = cums[c]
        b_ref = b[ref_row:ref_row + 1, :]
        b_tot = b[tot_row:tot_row + 1, :]
        q = q_ref[tok(c), q0:q0 + D_GLA_K] * (HEAD_K ** -0.5)
        k = k_ref[tok(c), k0:k0 + D_GLA_K]
        v = v_ref[tok(c), v0:v0 + D_GLA_V]
        qin_ref[stk(c), :] = stack_heads(q * jnp.exp(b - b_ref))
        kin_ref[tok(c), :] = (k * jnp.exp(b_ref - b)).astype(BF16)
        kdec_ref[stk(c), :] = stack_heads(k * jnp.exp(b_tot - b))
        qdec_ref[stk(c), :] = stack_heads(q * jnp.exp(b))
        v_st = jnp.concatenate([v[:, h * HEAD_V:(h + 1) * HEAD_V] for h in range(GLA_HEADS)], axis=0)
        vst_ref[stk(c), :] = v_st.astype(BF16)
        vstt_ref[vrows(c), :] = v_st.T.astype(BF16)
        decays.append(jnp.exp(b_tot))
        if interleave is not None:
            interleave("operands", c)

    for c in range(n_chunks):
        k_in = kin_ref[tok(c), :]
        att = _dot_nt(qin_ref[stk(c), :], jnp.concatenate([k_in] * GLA_HEADS, axis=0))
        att_ref[stk(c), :] = jnp.where(att_mask, att, 0.0).astype(BF16)
        kv_ref[vrows(c), :] = _dot(vstt_ref[vrows(c), :], kdec_ref[stk(c), :])
        if interleave is not None:
            interleave("scores", c)

    state = st_ref[...]
    for step in range(n_chunks):
        c = n_chunks - 1 - step if reverse else step
        o_st = _dot(att_ref[stk(c), :], vst_ref[stk(c), :]) + _dot_nt(qdec_ref[stk(c), :], state.astype(BF16))
        o_ref[tok(c), :] = jnp.concatenate(
            [o_st[h * CHUNK:(h + 1) * CHUNK, :] for h in range(GLA_HEADS)], axis=1)
        state = state * decays[c] + kv_ref[vrows(c), :]
    st_ref[...] = state


def _gla_bwd_kernel(x_ref, xnext_ref, wqk_ref, wv_ref, wlr_ref, w2_ref, b2_ref, o_ref, qkv_ref, laf_ref,
                    p0_ref, la0_ref, p1_ref, la1_ref, st_ref, *gla_scratch):
    step_no = pl.program_id(0) * pl.num_programs(1) + pl.program_id(1)
    pieces = [(w_ref, c0) for w_ref in (wqk_ref, wv_ref) for c0 in range(0, w_ref.shape[1], MXU_DIM)]

    def project_piece(xb, p_ref, j):
        w_ref, c0 = pieces[j]
        p_ref[:, j * MXU_DIM:(j + 1) * MXU_DIM] = _dot(xb, w_ref[:, c0:c0 + MXU_DIM])

    @pl.when(step_no == 0)
    def _():
        xb = x_ref[0].astype(BF16)
        la0_ref[...] = _log_decay(xb, wlr_ref, w2_ref, b2_ref)
        for j in range(len(pieces)):
            project_piece(xb, p0_ref, j)

    @pl.when(pl.program_id(1) == 0)
    def _():
        st_ref[...] = jnp.zeros(st_ref.shape, F32)

    n_chunks = x_ref.shape[1] // CHUNK
    n_pieces = len(pieces)
    stride = n_chunks // n_pieces
    assert stride >= 1 and (n_pieces - 1) * stride <= n_chunks - 1

    def run(p_ref, la_ref, p_next_ref, la_next_ref):
        xnb = xnext_ref[0].astype(BF16)
        la_next_ref[...] = _log_decay(xnb, wlr_ref, w2_ref, b2_ref)
        project_piece(xnb, p_next_ref, 0)

        def filler(stage, c):
            if stage == "operands":
                if c % stride == 0 and 1 <= c // stride < n_pieces:
                    project_piece(xnb, p_next_ref, c // stride)
            else:
                rows = slice(c * CHUNK, (c + 1) * CHUNK)
                qkv_ref[0, rows, :] = p_ref[rows, :]
                laf_ref[0, rows, :] = la_ref[rows, 0:D_GLA_K]

        _gla_tile(p_ref, p_ref, p_ref, la_ref, o_ref.at[0], st_ref, gla_scratch, q0=0, k0=D_GLA_K,
                  v0=2 * D_GLA_K, la0=D_GLA_K, reverse=True, interleave=filler)

    parity = step_no & 1

    @pl.when(parity == 0)
    def _():
        run(p0_ref, la0_ref, p1_ref, la1_ref)

    @pl.when(parity == 1)
    def _():
        run(p1_ref, la1_ref, p0_ref, la0_ref)


def _gla_bwd_call(x, w, l):
    bsz, seq, _ = x.shape
    nt = seq // TIME_TILE
    rev = lambda b, i: (b, nt - 1 - i, 0)

    def rev_next(b, i):
        nxt = jnp.minimum(b * nt + i + 1, bsz * nt - 1)
        return (nxt // nt, nt - 1 - nxt % nt, 0)

    return pl.pallas_call(
        _gla_bwd_kernel,
        grid=(bsz, nt),
        in_specs=[
            pl.BlockSpec((1, TIME_TILE, D_MODEL), rev),
            pl.BlockSpec((1, TIME_TILE, D_MODEL), rev_next),
            _weight_spec(w["w_in"], (l,), (D_MODEL, 2 * D_GLA_K), (0, _Q0 // (2 * D_GLA_K))),
            _weight_spec(w["w_in"], (l,), (D_MODEL, D_GLA_V), (0, _V0 // D_GLA_V)),
            _weight_spec(w["w_lr"], (l,)),
            _weight_spec(w["w2"], (l,)),
            _weight_spec(w["gate_b"], (l,)),
        ],
        out_specs=[
            pl.BlockSpec((1, TIME_TILE, D_GLA_V), rev),
            pl.BlockSpec((1, TIME_TILE, D_QKV), rev),
            pl.BlockSpec((1, TIME_TILE, D_GLA_K), rev),
        ],
        out_shape=[
            jax.ShapeDtypeStruct((bsz, seq, D_GLA_V), F32),
            jax.ShapeDtypeStruct((bsz, seq, D_QKV), F32),
            jax.ShapeDtypeStruct((bsz, seq, D_GLA_K), F32),
        ],
        scratch_shapes=[
            pltpu.VMEM((TIME_TILE, D_QKV), F32),
            pltpu.VMEM((TIME_TILE, 2 * D_GLA_K), F32),
            pltpu.VMEM((TIME_TILE, D_QKV), F32),
            pltpu.VMEM((TIME_TILE, 2 * D_GLA_K), F32),
            pltpu.VMEM((HEAD_V, D_GLA_K), F32),
        ] + _gla_scratch(TIME_TILE),
        compiler_params=pltpu.CompilerParams(
            dimension_semantics=("arbitrary", "arbitrary"), vmem_limit_bytes=VMEM_LIMIT_BYTES),
        name="gla_bwd",
    )(x, x, w["w_in"], w["w_in"], w["w_lr"], w["w2"], w["gate_b"])


def _mixer_kernel(x_ref, xprev_ref, xnext_ref, ob_ref, qkv_ref, laf_ref, wpc_ref, wpg_ref, wm_ref,
                  convw_ref, wco_ref, gn_ref, wgo_ref, wout_ref, g_ref, b_ref,
                  o_ref, p_ref, of_ref, mg_ref, yconv_ref, st_ref, *gla_scratch):
    i = pl.program_id(1)
    nt = pl.num_programs(1)

    @pl.when(i == 0)
    def _():
        st_ref[...] = jnp.zeros(st_ref.shape, F32)

    x = x_ref[0]
    xb = x.astype(BF16)
    rows = x.shape[0]
    x_edges = jnp.concatenate([xprev_ref[0], xnext_ref[0]], axis=0).astype(BF16)
    p_conv = _dot(jnp.concatenate([xb, x_edges], axis=0), wpc_ref[...])
    p_ref[:, 0:_Q0] = p_conv[0:rows, :]
    p_edges = p_conv[rows:, :]
    p_ref[:, _Q0:] = _dot(xb, wpg_ref[...])

    n_chunks = x.shape[0] // CHUNK
    piece = 2 * D_MODEL // n_chunks
    assert piece % MXU_DIM == 0

    def merge_gate_piece(step):
        cols = slice(step * piece, (step + 1) * piece)
        mg_ref[:, cols] = _dot(xb, wm_ref[:, cols])

    def conv_branch():
        u = p_ref[:, _GC0:_GC0 + D_CONV] * p_ref[:, _H0:_H0 + D_CONV]
        uh = p_edges[:, _GC0:_GC0 + D_CONV] * p_edges[:, _H0:_H0 + D_CONV]
        u_before = jnp.where(i > 0, uh[SUBLANES - 1:SUBLANES, :], 0.0)
        u_after = jnp.where(i < nt - 1, uh[SUBLANES:SUBLANES + 1, :], 0.0)
        n = u.shape[0]
        row = lax.broadcasted_iota(jnp.int32, (n, 1), 0)
        u_m1 = jnp.where(row == 0, u_before, pltpu.roll(u, 1, axis=0))
        u_p1 = jnp.where(row == n - 1, u_after, pltpu.roll(u, n - 1, axis=0))
        cw = convw_ref[...]
        conv = u_m1 * cw[0:1, :] + u * cw[1:2, :] + u_p1 * cw[2:3, :]
        yconv_ref[...] = _dot((p_ref[:, _GB0:_GB0 + D_CONV] * conv).astype(BF16), wco_ref[...])

    def dense_filler(stage, c):
        if stage == "operands":
            merge_gate_piece(c)
        elif c == 0:
            conv_branch()

    qkv = qkv_ref.at[0]
    _gla_tile(qkv, qkv, qkv, laf_ref.at[0], of_ref, st_ref, gla_scratch, q0=0, k0=D_GLA_K,
              v0=2 * D_GLA_K, la0=0, reverse=False, interleave=dense_filler)

    halves = [slice(0, rows // 2), slice(rows // 2, rows)]
    gn = gn_ref[...]
    o_glas = []
    for rs in halves:
        normed = []
        for h in range(GLA_HEADS):
            sl = slice(h * HEAD_V, (h + 1) * HEAD_V)
            o_h = of_ref[rs, sl] + ob_ref[0, rs, sl]
            ms = jnp.mean(o_h * o_h, axis=-1, keepdims=True)
            normed.append(o_h * lax.rsqrt(ms + RMS_EPS) * gn)
        gate = p_ref[rs, _Q0:_Q0 + D_GLA_V]
        o_glas.append((jnp.concatenate(normed, axis=1) * (gate * jax.nn.sigmoid(gate))).astype(BF16))
    y_glas = [_dot(o_gla, wgo_ref[...]) for o_gla in o_glas]
    mergeds = [(jax.nn.sigmoid(mg_ref[rs, :D_MODEL]) * yconv_ref[rs, :]
                + jax.nn.sigmoid(mg_ref[rs, D_MODEL:]) * y_gla).astype(BF16)
               for rs, y_gla in zip(halves, y_glas)]
    ys = [_dot(merged, wout_ref[...]) for merged in mergeds]
    for rs, y in zip(halves, ys):
        o_ref[0, rs, :] = _residual_layer_norm(x_ref[0, rs, :], y, g_ref[...], b_ref[...])


def _mixer_call(x, o_b, qkv, la_f, w, l):
    bsz, seq, _ = x.shape
    nt = seq // TIME_TILE
    per8 = TIME_TILE // SUBLANES
    last8 = seq // SUBLANES - 1
    tile = lambda b, i: (b, i, 0)
    return pl.pallas_call(
        _mixer_kernel,
        grid=(bsz, nt),
        in_specs=[
            pl.BlockSpec((1, TIME_TILE, D_MODEL), tile),
            pl.BlockSpec((1, SUBLANES, D_MODEL), lambda b, i: (b, jnp.maximum(i * per8 - 1, 0), 0)),
            pl.BlockSpec((1, SUBLANES, D_MODEL), lambda b, i: (b, jnp.minimum((i + 1) * per8, last8), 0)),
            pl.BlockSpec((1, TIME_TILE, D_GLA_V), tile),
            pl.BlockSpec((1, TIME_TILE, D_QKV), tile),
            pl.BlockSpec((1, TIME_TILE, D_GLA_K), tile),
            _weight_spec(w["w_in"], (l,), (D_MODEL, _Q0), (0, 0)),
            _weight_spec(w["w_in"], (l,), (D_MODEL, D_GLA_V), (0, _G0 // D_GLA_V)),
            _weight_spec(w["w_merge"], (l,)),
            _weight_spec(w["conv_w"], (l,)),
            _weight_spec(w["w_conv_out"], (l,)),
            _weight_spec(w["gla_norm_g"], (l,)),
            _weight_spec(w["w_gla_out"], (l,)),
            _weight_spec(w["w_mix_out"], (l,)),
            _weight_spec(w["ln_g"], (l, 1)),
            _weight_spec(w["ln_b"], (l, 1)),
        ],
        out_specs=pl.BlockSpec((1, TIME_TILE, D_MODEL), tile),
        out_shape=jax.ShapeDtypeStruct((bsz, seq, D_MODEL), F32),
        scratch_shapes=[
            pltpu.VMEM((TIME_TILE, _Q0 + D_GLA_V), F32),
            pltpu.VMEM((TIME_TILE, D_GLA_V), F32),
            pltpu.VMEM((TIME_TILE, 2 * D_MODEL), F32),
            pltpu.VMEM((TIME_TILE, D_MODEL), F32),
            pltpu.VMEM((HEAD_V, D_GLA_K), F32),
        ] + _gla_scratch(TIME_TILE),
        compiler_params=pltpu.CompilerParams(
            dimension_semantics=("arbitrary", "arbitrary"), vmem_limit_bytes=VMEM_LIMIT_BYTES),
        name="mixer_ln",
    )(x, x, x, o_b, qkv, la_f, w["w_in"], w["w_in"], w["w_merge"], w["conv_w"],
      w["w_conv_out"], w["gla_norm_g"], w["w_gla_out"], w["w_mix_out"], w["ln_g"], w["ln_b"])


def _xattn_kernel(x_ref, mem_ref, wq_ref, wkv_ref, wo_ref, g_ref, b_ref, o_ref, kv_ref):
    @pl.when(pl.program_id(1) == 0)
    def _():
        kv_ref[...] = _dot(mem_ref[0].astype(BF16), wkv_ref[...]).astype(BF16)

    subs = [slice(r0, r0 + XA_ROW_SUB) for r0 in range(0, x_ref.shape[1], XA_ROW_SUB)]
    head_cols = [slice(h * XA_HEAD_DIM, (h + 1) * XA_HEAD_DIM) for h in range(XA_HEADS)]
    qs = [_dot(x_ref[0, rs, :].astype(BF16), wq_ref[...]).astype(BF16) for rs in subs]
    scores = [[_dot_nt(q[:, hc], kv_ref[:, hc]) * (XA_HEAD_DIM ** -0.5) for hc in head_cols] for q in qs]
    outs = []
    for sub_scores in scores:
        heads = []
        for h, s in enumerate(sub_scores):
            v_h = kv_ref[:, D_MODEL + h * XA_HEAD_DIM:D_MODEL + (h + 1) * XA_HEAD_DIM]
            e = jnp.exp(s - jnp.max(s, axis=-1, keepdims=True))
            p = e / jnp.sum(e, axis=-1, keepdims=True)
            heads.append(_dot(p.astype(BF16), v_h))
        outs.append(jnp.concatenate(heads, axis=1).astype(BF16))
    ys = [_dot(o, wo_ref[...]) for o in outs]
    for rs, y in zip(subs, ys):
        o_ref[0, rs, :] = _residual_layer_norm(x_ref[0, rs, :], y, g_ref[...], b_ref[...])


def _xattn_call(x, mem, w, l):
    bsz, seq, _ = x.shape
    n_mem = mem.shape[1]
    tile = lambda b, i: (b, i, 0)
    return pl.pallas_call(
        _xattn_kernel,
        grid=(bsz, seq // XA_TILE),
        in_specs=[
            pl.BlockSpec((1, XA_TILE, D_MODEL), tile),
            pl.BlockSpec((1, n_mem, D_MODEL), lambda b, i: (b, 0, 0)),
            _weight_spec(w["xa_w_q"], (l,)),
            _weight_spec(w["xa_w_kv"], (l,)),
            _weight_spec(w["xa_w_o"], (l,)),
            _weight_spec(w["ln_g"], (l, 2)),
            _weight_spec(w["ln_b"], (l, 2)),
        ],
        out_specs=pl.BlockSpec((1, XA_TILE, D_MODEL), tile),
        out_shape=jax.ShapeDtypeStruct((bsz, seq, D_MODEL), F32),
        scratch_shapes=[pltpu.VMEM((n_mem, 2 * D_MODEL), BF16)],
        compiler_params=pltpu.CompilerParams(
            dimension_semantics=("arbitrary", "arbitrary"), vmem_limit_bytes=VMEM_LIMIT_BYTES),
        name="xattn_ln",
    )(x, mem, w["xa_w_q"], w["xa_w_kv"], w["xa_w_o"], w["ln_g"], w["ln_b"])


def _prepare_weights(ffn1_w_gu, ffn1_w_down, w_mix_in, conv_w, w_conv_out, gla_gate_w2, gla_gate_b,
                     gla_norm_g, w_gla_out, w_mix_out, xa_w_q, xa_w_kv, xa_w_o, ffn2_w_gu, ffn2_w_down,
                     ln_g, ln_b):
    bf = lambda a: a.astype(BF16)
    w_in = bf(w_mix_in)
    lr_pad = LR_PAD - 2 * GATE_RANK
    w2 = bf(gla_gate_w2)
    return dict(
        ffn1_w_gu=bf(ffn1_w_gu), ffn1_w_down=bf(ffn1_w_down),
        ffn2_w_gu=bf(ffn2_w_gu), ffn2_w_down=bf(ffn2_w_down),
        w_in=w_in,
        w_lr=jnp.pad(w_in[:, :, _LR0:_MERGE0], ((0, 0), (0, 0), (0, lr_pad))),
        w2=jnp.concatenate([jnp.pad(w2[:, 0], ((0, 0), (0, LR_PAD - GATE_RANK), (0, 0))),
                            jnp.pad(w2[:, 1], ((0, 0), (GATE_RANK, lr_pad), (0, 0)))], axis=-1),
        gate_b=gla_gate_b.reshape(gla_gate_b.shape[0], 1, 2 * D_GLA_K),
        w_merge=w_in[:, :, _MERGE0:], conv_w=conv_w, w_conv_out=bf(w_conv_out),
        gla_norm_g=gla_norm_g[:, None, :], w_gla_out=bf(w_gla_out), w_mix_out=bf(w_mix_out),
        xa_w_q=bf(xa_w_q), xa_w_kv=bf(xa_w_kv), xa_w_o=bf(xa_w_o),
        ln_g=ln_g[:, :, None, :], ln_b=ln_b[:, :, None, :],
    )


def _trunk(x, mem, w):
    bsz, seq, d = x.shape
    for l in range(DEPTH):
        x = _ffn_call(x.reshape(bsz * seq, d), w["ffn1_w_gu"], w["ffn1_w_down"], w, l, 0).reshape(bsz, seq, d)
        o_b, qkv, la_f = _gla_bwd_call(x, w, l)
        x = _mixer_call(x, o_b, qkv, la_f, w, l)
        x = _xattn_call(x, mem, w, l)
        x = _ffn_call(x.reshape(bsz * seq, d), w["ffn2_w_gu"], w["ffn2_w_down"], w, l, 3).reshape(bsz, seq, d)
    return x


def kernel(x_prompt, x_sample, mem_prompt, mem_sample, ffn1_w_gu, ffn1_w_down, w_mix_in, conv_w, w_conv_out, gla_gate_w2, gla_gate_b, gla_norm_g, w_gla_out, w_mix_out, xa_w_q, xa_w_kv, xa_w_o, ffn2_w_gu, ffn2_w_down, ln_g, ln_b):
    w = _prepare_weights(ffn1_w_gu, ffn1_w_down, w_mix_in, conv_w, w_conv_out, gla_gate_w2, gla_gate_b,
                         gla_norm_g, w_gla_out, w_mix_out, xa_w_q, xa_w_kv, xa_w_o, ffn2_w_gu,
                         ffn2_w_down, ln_g, ln_b)
    return (_trunk(x_prompt, mem_prompt, w), _trunk(x_sample, mem_sample, w))
```

```python
import jax
import jax.numpy as jnp
from jax import lax
from jax.experimental import pallas as pl
from jax.experimental.pallas import tpu as pltpu

F32 = jnp.float32
BF16 = jnp.bfloat16

D_MODEL = 1024
DEPTH = 4
D_FF = 2816
D_CONV = D_MODEL // 2
GLA_HEADS = 4
D_GLA_V = D_MODEL // 2
HEAD_V = D_GLA_V // GLA_HEADS
D_GLA_K = D_GLA_V // 2
HEAD_K = D_GLA_K // GLA_HEADS
GATE_RANK = 16
GATE_TEMP = 16.0
CHUNK = 64
XA_HEADS = 4
XA_HEAD_DIM = D_MODEL // XA_HEADS
LN_EPS = 1e-5
RMS_EPS = 1e-6
DN_ALPHA = (2 * DEPTH) ** 0.25

SUBLANES = 8
LANES = 128
V7X_VMEM_BYTES = 64 * 1024 * 1024
VMEM_LIMIT_BYTES = V7X_VMEM_BYTES * 7 // 8

TOKEN_TILE = 1024
TIME_TILE = 512
MXU_DIM = 256
XA_TILE = 1024
XA_ROW_SUB = 256
FFN_ROW_SUBS = (256, 256, 256, 256)
assert sum(FFN_ROW_SUBS) == TOKEN_TILE
FF_CHUNKS = (6 * MXU_DIM, 5 * MXU_DIM)
assert sum(FF_CHUNKS) == D_FF
LR_PAD = LANES

_H0, _GB0, _GC0 = 0, D_CONV, 2 * D_CONV
_Q0 = 3 * D_CONV
_K0 = _Q0 + D_GLA_K
_V0 = _K0 + D_GLA_K
_G0 = _V0 + D_GLA_V
D_QKV = 2 * D_GLA_K + D_GLA_V
_LR0 = _G0 + D_GLA_V
_MERGE0 = _LR0 + 2 * GATE_RANK
assert _Q0 % (2 * D_GLA_K) == 0 and _V0 % D_GLA_V == 0 and _G0 % D_GLA_V == 0


def _dot(a, b):
    return jnp.dot(a, b, preferred_element_type=F32)


def _dot_nt(a, b):
    return lax.dot_general(a, b, (((1,), (1,)), ((), ())), preferred_element_type=F32)


def _residual_layer_norm(x, y, g, b):
    z = DN_ALPHA * x + y
    mu = jnp.mean(z, axis=-1, keepdims=True)
    zc = z - mu
    var = jnp.mean(zc * zc, axis=-1, keepdims=True)
    return zc * lax.rsqrt(var + LN_EPS) * g + b


def _weight_spec(arr, lead, block=None, index=None):
    lead = tuple(lead)
    tail = tuple(arr.shape[len(lead):]) if block is None else tuple(block)
    pos = (0,) * len(tail) if index is None else tuple(index)
    return pl.BlockSpec((None,) * len(lead) + tail, lambda *_: lead + pos, pipeline_mode=pl.Buffered(1))


def _ffn_kernel(x_ref, wgu_ref, wd_ref, g_ref, b_ref, o_ref):
    r0 = 0
    for sub in FFN_ROW_SUBS:
        x = x_ref[r0:r0 + sub, :]
        xb = x.astype(BF16)
        acc = None
        c0 = 0
        for width in FF_CHUNKS:
            gate = _dot(xb, wgu_ref[:, c0:c0 + width])
            up = _dot(xb, wgu_ref[:, D_FF + c0:D_FF + c0 + width])
            act = (gate * jax.nn.sigmoid(gate)) * up
            part = _dot(act.astype(BF16), wd_ref[c0:c0 + width, :])
            acc = part if acc is None else acc + part
            c0 += width
        o_ref[r0:r0 + sub, :] = _residual_layer_norm(x, 0.5 * acc, g_ref[...], b_ref[...])
        r0 += sub


def _ffn_call(x2d, w_gu, w_down, w, l, ln_idx):
    n = x2d.shape[0]
    assert n % TOKEN_TILE == 0
    return pl.pallas_call(
        _ffn_kernel,
        grid=(n // TOKEN_TILE,),
        in_specs=[
            pl.BlockSpec((TOKEN_TILE, D_MODEL), lambda i: (i, 0)),
            _weight_spec(w_gu, (l,)),
            _weight_spec(w_down, (l,)),
            _weight_spec(w["ln_g"], (l, ln_idx)),
            _weight_spec(w["ln_b"], (l, ln_idx)),
        ],
        out_specs=pl.BlockSpec((TOKEN_TILE, D_MODEL), lambda i: (i, 0)),
        out_shape=jax.ShapeDtypeStruct((n, D_MODEL), F32),
        compiler_params=pltpu.CompilerParams(
            dimension_semantics=("arbitrary",), vmem_limit_bytes=VMEM_LIMIT_BYTES),
        name="ffn_ln",
    )(x2d, w_gu, w_down, w["ln_g"], w["ln_b"])


def _log_decay(xb, wlr_ref, w2_ref, b2_ref):
    lr = _dot(xb, wlr_ref[...])
    z = _dot(lr.astype(BF16), w2_ref[...]) + b2_ref[...]
    return jax.nn.log_sigmoid(z) / GATE_TEMP


def _gla_scratch(rows):
    n = rows // CHUNK
    stacked = GLA_HEADS * CHUNK
    return [
        pltpu.VMEM((n * stacked, D_GLA_K), BF16),
        pltpu.VMEM((n * stacked, D_GLA_K), BF16),
        pltpu.VMEM((n * stacked, D_GLA_K), BF16),
        pltpu.VMEM((rows, D_GLA_K), BF16),
        pltpu.VMEM((n * stacked, HEAD_V), BF16),
        pltpu.VMEM((n * HEAD_V, stacked), BF16),
        pltpu.VMEM((n * stacked, stacked), BF16),
        pltpu.VMEM((n * HEAD_V, D_GLA_K), F32),
    ]


def _gla_tile(q_ref, k_ref, v_ref, la_ref, o_ref, st_ref, scratch, *, q0, k0, v0, la0, reverse,
              interleave=None):
    qin_ref, qdec_ref, kdec_ref, kin_ref, vst_ref, vstt_ref, att_ref, kv_ref = scratch
    rows = q_ref.shape[0]
    n_chunks = rows // CHUNK
    ri = lax.broadcasted_iota(jnp.int32, (CHUNK, CHUNK), 0)
    ci = lax.broadcasted_iota(jnp.int32, (CHUNK, CHUNK), 1)
    stacked = GLA_HEADS * CHUNK
    sr = lax.broadcasted_iota(jnp.int32, (stacked, stacked), 0)
    sc = lax.broadcasted_iota(jnp.int32, (stacked, stacked), 1)
    same_head = (sr // CHUNK) == (sc // CHUNK)
    if reverse:
        cum = (ci >= ri).astype(BF16)
        att_mask = same_head & ((sc % CHUNK) > (sr % CHUNK))
        ref_row, tot_row = CHUNK // 2, 0
    else:
        cum = (ci <= ri).astype(BF16)
        att_mask = same_head & ((sc % CHUNK) <= (sr % CHUNK))
        ref_row, tot_row = CHUNK // 2 - 1, CHUNK - 1
    head_of_lane = lax.broadcasted_iota(jnp.int32, (1, D_GLA_K), 1) // HEAD_K
    head_masks = [(head_of_lane == h).astype(BF16) for h in range(GLA_HEADS)]

    def stack_heads(a):
        ab = a.astype(BF16)
        return jnp.concatenate([ab * m for m in head_masks], axis=0)

    tok = lambda c: slice(c * CHUNK, (c + 1) * CHUNK)
    stk = lambda c: slice(c * stacked, (c + 1) * stacked)
    vrows = lambda c: slice(c * HEAD_V, (c + 1) * HEAD_V)

    cums = []
    for c in range(n_chunks):
        la = la_ref[tok(c), la0:la0 + D_GLA_K]
        la_hi = la.astype(BF16)
        la_lo = (la - la_hi.astype(F32)).astype(BF16)
        cums.append(_dot(cum, la_hi) + _dot(cum, la_lo))

    decays = []
    for c in range(n_chunks):
        b = cums[c]
        b_ref = b[ref_row:ref_row + 1, :]
        b_tot = b[tot_row:tot_row + 1, :]
        q = q_ref[tok(c), q0:q0 + D_GLA_K] * (HEAD_K ** -0.5)
        k = k_ref[tok(c), k0:k0 + D_GLA_K]
        v = v_ref[tok(c), v0:v0 + D_GLA_V]
        qin_ref[stk(c), :] = stack_heads(q * jnp.exp(b - b_ref))
        kin_ref[tok(c), :] = (k * jnp.exp(b_ref - b)).astype(BF16)
        kdec_ref[stk(c), :] = stack_heads(k * jnp.exp(b_tot - b))
        qdec_ref[stk(c), :] = stack_heads(q * jnp.exp(b))
        v_st = jnp.concatenate([v[:, h * HEAD_V:(h + 1) * HEAD_V] for h in range(GLA_HEADS)], axis=0)
        vst_ref[stk(c), :] = v_st.astype(BF16)
        vstt_ref[vrows(c), :] = v_st.T.astype(BF16)
        decays.append(jnp.exp(b_tot))
        if interleave is not None:
            interleave("operands", c)

    for c in range(n_chunks):
        k_in = kin_ref[tok(c), :]
        att = _dot_nt(qin_ref[stk(c), :], jnp.concatenate([k_in] * GLA_HEADS, axis=0))
        att_ref[stk(c), :] = jnp.where(att_mask, att, 0.0).astype(BF16)
        kv_ref[vrows(c), :] = _dot(vstt_ref[vrows(c), :], kdec_ref[stk(c), :])
        if interleave is not None:
            interleave("scores", c)

    state = st_ref[...]
    for step in range(n_chunks):
        c = n_chunks - 1 - step if reverse else step
        o_st = _dot(att_ref[stk(c), :], vst_ref[stk(c), :]) + _dot_nt(qdec_ref[stk(c), :], state.astype(BF16))
        o_ref[tok(c), :] = jnp.concatenate(
            [o_st[h * CHUNK:(h + 1) * CHUNK, :] for h in range(GLA_HEADS)], axis=1)
        state = state * decays[c] + kv_ref[vrows(c), :]
    st_ref[...] = state


def _gla_bwd_kernel(x_ref, xnext_ref, wqk_ref, wv_ref, wlr_ref, w2_ref, b2_ref, o_ref, qkv_ref, laf_ref,
                    p0_ref, la0_ref, p1_ref, la1_ref, st_ref, *gla_scratch):
    step_no = pl.program_id(0) * pl.num_programs(1) + pl.program_id(1)
    pieces = [(w_ref, c0) for w_ref in (wqk_ref, wv_ref) for c0 in range(0, w_ref.shape[1], MXU_DIM)]

    def project_piece(xb, p_ref, j):
        w_ref, c0 = pieces[j]
        p_ref[:, j * MXU_DIM:(j + 1) * MXU_DIM] = _dot(xb, w_ref[:, c0:c0 + MXU_DIM])

    @pl.when(step_no == 0)
    def _():
        xb = x_ref[0].astype(BF16)
        la0_ref[...] = _log_decay(xb, wlr_ref, w2_ref, b2_ref)
        for j in range(len(pieces)):
            project_piece(xb, p0_ref, j)

    @pl.when(pl.program_id(1) == 0)
    def _():
        st_ref[...] = jnp.zeros(st_ref.shape, F32)

    n_chunks = x_ref.shape[1] // CHUNK
    n_pieces = len(pieces)
    stride = n_chunks // n_pieces
    assert stride >= 1 and 1 + (n_pieces - 1) * stride <= n_chunks - 1

    def run(p_ref, la_ref, p_next_ref, la_next_ref):
        xnb = xnext_ref[0].astype(BF16)

        def project_next_piece(stage, step):
            if stage != "operands":
                return
            if step == 0:
                la_next_ref[...] = _log_decay(xnb, wlr_ref, w2_ref, b2_ref)
            elif (step - 1) % stride == 0 and (step - 1) // stride < n_pieces:
                project_piece(xnb, p_next_ref, (step - 1) // stride)

        qkv_ref[0] = p_ref[...]
        laf_ref[0] = la_ref[:, 0:D_GLA_K]
        _gla_tile(p_ref, p_ref, p_ref, la_ref, o_ref.at[0], st_ref, gla_scratch, q0=0, k0=D_GLA_K,
                  v0=2 * D_GLA_K, la0=D_GLA_K, reverse=True, interleave=project_next_piece)

    parity = step_no & 1

    @pl.when(parity == 0)
    def _():
        run(p0_ref, la0_ref, p1_ref, la1_ref)

    @pl.when(parity == 1)
    def _():
        run(p1_ref, la1_ref, p0_ref, la0_ref)


def _gla_bwd_call(x, w, l):
    bsz, seq, _ = x.shape
    nt = seq // TIME_TILE
    rev = lambda b, i: (b, nt - 1 - i, 0)

    def rev_next(b, i):
        nxt = jnp.minimum(b * nt + i + 1, bsz * nt - 1)
        return (nxt // nt, nt - 1 - nxt % nt, 0)

    return pl.pallas_call(
        _gla_bwd_kernel,
        grid=(bsz, nt),
        in_specs=[
            pl.BlockSpec((1, TIME_TILE, D_MODEL), rev),
            pl.BlockSpec((1, TIME_TILE, D_MODEL), rev_next),
            _weight_spec(w["w_in"], (l,), (D_MODEL, 2 * D_GLA_K), (0, _Q0 // (2 * D_GLA_K))),
            _weight_spec(w["w_in"], (l,), (D_MODEL, D_GLA_V), (0, _V0 // D_GLA_V)),
            _weight_spec(w["w_lr"], (l,)),
            _weight_spec(w["w2"], (l,)),
            _weight_spec(w["gate_b"], (l,)),
        ],
        out_specs=[
            pl.BlockSpec((1, TIME_TILE, D_GLA_V), rev),
            pl.BlockSpec((1, TIME_TILE, D_QKV), rev),
            pl.BlockSpec((1, TIME_TILE, D_GLA_K), rev),
        ],
        out_shape=[
            jax.ShapeDtypeStruct((bsz, seq, D_GLA_V), F32),
            jax.ShapeDtypeStruct((bsz, seq, D_QKV), F32),
            jax.ShapeDtypeStruct((bsz, seq, D_GLA_K), F32),
        ],
        scratch_shapes=[
            pltpu.VMEM((TIME_TILE, D_QKV), F32),
            pltpu.VMEM((TIME_TILE, 2 * D_GLA_K), F32),
            pltpu.VMEM((TIME_TILE, D_QKV), F32),
            pltpu.VMEM((TIME_TILE, 2 * D_GLA_K), F32),
            pltpu.VMEM((HEAD_V, D_GLA_K), F32),
        ] + _gla_scratch(TIME_TILE),
        compiler_params=pltpu.CompilerParams(
            dimension_semantics=("arbitrary", "arbitrary"), vmem_limit_bytes=VMEM_LIMIT_BYTES),
        name="gla_bwd",
    )(x, x, w["w_in"], w["w_in"], w["w_lr"], w["w2"], w["gate_b"])


def _mixer_kernel(x_ref, xprev_ref, xnext_ref, ob_ref, qkv_ref, laf_ref, wpc_ref, wpg_ref, wm_ref,
                  convw_ref, wco_ref, gn_ref, wgo_ref, wout_ref, g_ref, b_ref,
                  o_ref, p_ref, of_ref, mg_ref, yconv_ref, st_ref, *gla_scratch):
    i = pl.program_id(1)
    nt = pl.num_programs(1)

    @pl.when(i == 0)
    def _():
        st_ref[...] = jnp.zeros(st_ref.shape, F32)

    x = x_ref[0]
    xb = x.astype(BF16)
    rows = x.shape[0]
    x_edges = jnp.concatenate([xprev_ref[0], xnext_ref[0]], axis=0).astype(BF16)
    p_conv = _dot(jnp.concatenate([xb, x_edges], axis=0), wpc_ref[...])
    p_ref[:, 0:_Q0] = p_conv[0:rows, :]
    p_edges = p_conv[rows:, :]
    p_ref[:, _Q0:] = _dot(xb, wpg_ref[...])

    n_chunks = x.shape[0] // CHUNK
    piece = 2 * D_MODEL // n_chunks
    assert piece % MXU_DIM == 0

    def merge_gate_piece(step):
        cols = slice(step * piece, (step + 1) * piece)
        mg_ref[:, cols] = _dot(xb, wm_ref[:, cols])

    def conv_branch():
        u = p_ref[:, _GC0:_GC0 + D_CONV] * p_ref[:, _H0:_H0 + D_CONV]
        uh = p_edges[:, _GC0:_GC0 + D_CONV] * p_edges[:, _H0:_H0 + D_CONV]
        u_before = jnp.where(i > 0, uh[SUBLANES - 1:SUBLANES, :], 0.0)
        u_after = jnp.where(i < nt - 1, uh[SUBLANES:SUBLANES + 1, :], 0.0)
        n = u.shape[0]
        row = lax.broadcasted_iota(jnp.int32, (n, 1), 0)
        u_m1 = jnp.where(row == 0, u_before, pltpu.roll(u, 1, axis=0))
        u_p1 = jnp.where(row == n - 1, u_after, pltpu.roll(u, n - 1, axis=0))
        cw = convw_ref[...]
        conv = u_m1 * cw[0:1, :] + u * cw[1:2, :] + u_p1 * cw[2:3, :]
        yconv_ref[...] = _dot((p_ref[:, _GB0:_GB0 + D_CONV] * conv).astype(BF16), wco_ref[...])

    def dense_filler(stage, c):
        if stage == "operands":
            merge_gate_piece(c)
        elif c == 0:
            conv_branch()

    qkv = qkv_ref.at[0]
    _gla_tile(qkv, qkv, qkv, laf_ref.at[0], of_ref, st_ref, gla_scratch, q0=0, k0=D_GLA_K,
              v0=2 * D_GLA_K, la0=0, reverse=False, interleave=dense_filler)

    halves = [slice(0, rows // 2), slice(rows // 2, rows)]
    gn = gn_ref[...]
    o_glas = []
    for rs in halves:
        normed = []
        for h in range(GLA_HEADS):
            sl = slice(h * HEAD_V, (h + 1) * HEAD_V)
            o_h = of_ref[rs, sl] + ob_ref[0, rs, sl]
            ms = jnp.mean(o_h * o_h, axis=-1, keepdims=True)
            normed.append(o_h * lax.rsqrt(ms + RMS_EPS) * gn)
        gate = p_ref[rs, _Q0:_Q0 + D_GLA_V]
        o_glas.append((jnp.concatenate(normed, axis=1) * (gate * jax.nn.sigmoid(gate))).astype(BF16))
    y_glas = [_dot(o_gla, wgo_ref[...]) for o_gla in o_glas]
    mergeds = [(jax.nn.sigmoid(mg_ref[rs, :D_MODEL]) * yconv_ref[rs, :]
                + jax.nn.sigmoid(mg_ref[rs, D_MODEL:]) * y_gla).astype(BF16)
               for rs, y_gla in zip(halves, y_glas)]
    ys = [_dot(merged, wout_ref[...]) for merged in mergeds]
    for rs, y in zip(halves, ys):
        o_ref[0, rs, :] = _residual_layer_norm(x_ref[0, rs, :], y, g_ref[...], b_ref[...])


def _mixer_call(x, o_b, qkv, la_f, w, l):
    bsz, seq, _ = x.shape
    nt = seq // TIME_TILE
    per8 = TIME_TILE // SUBLANES
    last8 = seq // SUBLANES - 1
    tile = lambda b, i: (b, i, 0)
    return pl.pallas_call(
        _mixer_kernel,
        grid=(bsz, nt),
        in_specs=[
            pl.BlockSpec((1, TIME_TILE, D_MODEL), tile),
            pl.BlockSpec((1, SUBLANES, D_MODEL), lambda b, i: (b, jnp.maximum(i * per8 - 1, 0), 0)),
            pl.BlockSpec((1, SUBLANES, D_MODEL), lambda b, i: (b, jnp.minimum((i + 1) * per8, last8), 0)),
            pl.BlockSpec((1, TIME_TILE, D_GLA_V), tile),
            pl.BlockSpec((1, TIME_TILE, D_QKV), tile),
            pl.BlockSpec((1, TIME_TILE, D_GLA_K), tile),
            _weight_spec(w["w_in"], (l,), (D_MODEL, _Q0), (0, 0)),
            _weight_spec(w["w_in"], (l,), (D_MODEL, D_GLA_V), (0, _G0 // D_GLA_V)),
            _weight_spec(w["w_merge"], (l,)),
            _weight_spec(w["conv_w"], (l,)),
            _weight_spec(w["w_conv_out"], (l,)),
            _weight_spec(w["gla_norm_g"], (l,)),
            _weight_spec(w["w_gla_out"], (l,)),
            _weight_spec(w["w_mix_out"], (l,)),
            _weight_spec(w["ln_g"], (l, 1)),
            _weight_spec(w["ln_b"], (l, 1)),
        ],
        out_specs=pl.BlockSpec((1, TIME_TILE, D_MODEL), tile),
        out_shape=jax.ShapeDtypeStruct((bsz, seq, D_MODEL), F32),
        scratch_shapes=[
            pltpu.VMEM((TIME_TILE, _Q0 + D_GLA_V), F32),
            pltpu.VMEM((TIME_TILE, D_GLA_V), F32),
            pltpu.VMEM((TIME_TILE, 2 * D_MODEL), F32),
            pltpu.VMEM((TIME_TILE, D_MODEL), F32),
            pltpu.VMEM((HEAD_V, D_GLA_K), F32),
        ] + _gla_scratch(TIME_TILE),
        compiler_params=pltpu.CompilerParams(
            dimension_semantics=("arbitrary", "arbitrary"), vmem_limit_bytes=VMEM_LIMIT_BYTES),
        name="mixer_ln",
    )(x, x, x, o_b, qkv, la_f, w["w_in"], w["w_in"], w["w_merge"], w["conv_w"],
      w["w_conv_out"], w["gla_norm_g"], w["w_gla_out"], w["w_mix_out"], w["ln_g"], w["ln_b"])


def _xattn_kernel(x_ref, mem_ref, wq_ref, wkv_ref, wo_ref, g_ref, b_ref, o_ref, kv_ref):
    @pl.when(pl.program_id(1) == 0)
    def _():
        kv_ref[...] = _dot(mem_ref[0].astype(BF16), wkv_ref[...]).astype(BF16)

    subs = [slice(r0, r0 + XA_ROW_SUB) for r0 in range(0, x_ref.shape[1], XA_ROW_SUB)]
    head_cols = [slice(h * XA_HEAD_DIM, (h + 1) * XA_HEAD_DIM) for h in range(XA_HEADS)]
    qs = [_dot(x_ref[0, rs, :].astype(BF16), wq_ref[...]).astype(BF16) for rs in subs]
    scores = [[_dot_nt(q[:, hc], kv_ref[:, hc]) * (XA_HEAD_DIM ** -0.5) for hc in head_cols] for q in qs]
    outs = []
    for sub_scores in scores:
        heads = []
        for h, s in enumerate(sub_scores):
            v_h = kv_ref[:, D_MODEL + h * XA_HEAD_DIM:D_MODEL + (h + 1) * XA_HEAD_DIM]
            e = jnp.exp(s - jnp.max(s, axis=-1, keepdims=True))
            p = e / jnp.sum(e, axis=-1, keepdims=True)
            heads.append(_dot(p.astype(BF16), v_h))
        outs.append(jnp.concatenate(heads, axis=1).astype(BF16))
    ys = [_dot(o, wo_ref[...]) for o in outs]
    for rs, y in zip(subs, ys):
        o_ref[0, rs, :] = _residual_layer_norm(x_ref[0, rs, :], y, g_ref[...], b_ref[...])


def _xattn_call(x, mem, w, l):
    bsz, seq, _ = x.shape
    n_mem = mem.shape[1]
    tile = lambda b, i: (b, i, 0)
    return pl.pallas_call(
        _xattn_kernel,
        grid=(bsz, seq // XA_TILE),
        in_specs=[
            pl.BlockSpec((1, XA_TILE, D_MODEL), tile),
            pl.BlockSpec((1, n_mem, D_MODEL), lambda b, i: (b, 0, 0)),
            _weight_spec(w["xa_w_q"], (l,)),
            _weight_spec(w["xa_w_kv"], (l,)),
            _weight_spec(w["xa_w_o"], (l,)),
            _weight_spec(w["ln_g"], (l, 2)),
            _weight_spec(w["ln_b"], (l, 2)),
        ],
        out_specs=pl.BlockSpec((1, XA_TILE, D_MODEL), tile),
        out_shape=jax.ShapeDtypeStruct((bsz, seq, D_MODEL), F32),
        scratch_shapes=[pltpu.VMEM((n_mem, 2 * D_MODEL), BF16)],
        compiler_params=pltpu.CompilerParams(
            dimension_semantics=("arbitrary", "arbitrary"), vmem_limit_bytes=VMEM_LIMIT_BYTES),
        name="xattn_ln",
    )(x, mem, w["xa_w_q"], w["xa_w_kv"], w["xa_w_o"], w["ln_g"], w["ln_b"])


def _prepare_weights(ffn1_w_gu, ffn1_w_down, w_mix_in, conv_w, w_conv_out, gla_gate_w2, gla_gate_b,
                     gla_norm_g, w_gla_out, w_mix_out, xa_w_q, xa_w_kv, xa_w_o, ffn2_w_gu, ffn2_w_down,
                     ln_g, ln_b):
    bf = lambda a: a.astype(BF16)
    w_in = bf(w_mix_in)
    lr_pad = LR_PAD - 2 * GATE_RANK
    w2 = bf(gla_gate_w2)
    return dict(
        ffn1_w_gu=bf(ffn1_w_gu), ffn1_w_down=bf(ffn1_w_down),
        ffn2_w_gu=bf(ffn2_w_gu), ffn2_w_down=bf(ffn2_w_down),
        w_in=w_in,
        w_lr=jnp.pad(w_in[:, :, _LR0:_MERGE0], ((0, 0), (0, 0), (0, lr_pad))),
        w2=jnp.concatenate([jnp.pad(w2[:, 0], ((0, 0), (0, LR_PAD - GATE_RANK), (0, 0))),
                            jnp.pad(w2[:, 1], ((0, 0), (GATE_RANK, lr_pad), (0, 0)))], axis=-1),
        gate_b=gla_gate_b.reshape(gla_gate_b.shape[0], 1, 2 * D_GLA_K),
        w_merge=w_in[:, :, _MERGE0:], conv_w=conv_w, w_conv_out=bf(w_conv_out),
        gla_norm_g=gla_norm_g[:, None, :], w_gla_out=bf(w_gla_out), w_mix_out=bf(w_mix_out),
        xa_w_q=bf(xa_w_q), xa_w_kv=bf(xa_w_kv), xa_w_o=bf(xa_w_o),
        ln_g=ln_g[:, :, None, :], ln_b=ln_b[:, :, None, :],
    )


def _trunk(x, mem, w):
    bsz, seq, d = x.shape
    for l in range(DEPTH):
        x = _ffn_call(x.reshape(bsz * seq, d), w["ffn1_w_gu"], w["ffn1_w_down"], w, l, 0).reshape(bsz, seq, d)
        o_b, qkv, la_f = _gla_bwd_call(x, w, l)
        x = _mixer_call(x, o_b, qkv, la_f, w, l)
        x = _xattn_call(x, mem, w, l)
        x = _ffn_call(x.reshape(bsz * seq, d), w["ffn2_w_gu"], w["ffn2_w_down"], w, l, 3).reshape(bsz, seq, d)
    return x


def kernel(x_prompt, x_sample, mem_prompt, mem_sample, ffn1_w_gu, ffn1_w_down, w_mix_in, conv_w, w_conv_out, gla_gate_w2, gla_gate_b, gla_norm_g, w_gla_out, w_mix_out, xa_w_q, xa_w_kv, xa_w_o, ffn2_w_gu, ffn2_w_down, ln_g, ln_b):
    w = _prepare_weights(ffn1_w_gu, ffn1_w_down, w_mix_in, conv_w, w_conv_out, gla_gate_w2, gla_gate_b,
                         gla_norm_g, w_gla_out, w_mix_out, xa_w_q, xa_w_kv, xa_w_o, ffn2_w_gu,
                         ffn2_w_down, ln_g, ln_b)
    return (_trunk(x_prompt, mem_prompt, w), _trunk(x_sample, mem_sample, w))
```
